```python
import jax, jax.numpy as jnp
from jax import lax
import numpy as np

D_MODEL = 1024
BATCH = 8
SEQ = 4096
DEPTH = 1
DEC_BATCH = 32
DEC_SEQ = 8
PAST_LEN = 16384
PAGE_SIZE = 128

CONV_CH = D_MODEL
CONV_WIDTH = 31
N_SLOTS = 8
HEAD_DIM = 64
GROUPS = ((128, 1), (512, 4), (2048, 16))
N_GROUPS = len(GROUPS)
ATT_W = N_SLOTS * HEAD_DIM
QKV_W = N_GROUPS * ATT_W
Q_BLK = 128
ALPHA = (2.0 * DEPTH) ** 0.25
BETA = (8.0 * DEPTH) ** -0.25
LN_EPS = 1e-5
SPLITS = (CONV_CH, CONV_CH, CONV_CH, QKV_W, QKV_W, QKV_W, ATT_W, D_MODEL, D_MODEL)
IN_W = sum(SPLITS)

kernel_name = "gated_conformer_dilated_alibi_deepnorm_step"


def layer_norm(x, g, b):
    xf = x.astype(jnp.float32)
    mu = xf.mean(-1, keepdims=True)
    var = jnp.square(xf - mu).mean(-1, keepdims=True)
    y = (xf - mu) * lax.rsqrt(var + LN_EPS) * g.astype(jnp.float32) + b.astype(jnp.float32)
    return y.astype(x.dtype)


def alibi_slopes():
    return 2.0 ** (-8.0 * jnp.arange(1, N_SLOTS + 1, dtype=jnp.float32) / N_SLOTS)


def in_projection(x, w_in, b_in):
    h = jnp.einsum('bsd,de->bse', x, w_in) + b_in
    offs = np.cumsum(SPLITS)[:-1].tolist()
    a_val, a_gate, z_a, q, k, v, z_b, g_a, g_b = jnp.split(h, offs, axis=-1)
    u = a_val * jax.nn.sigmoid(a_gate)
    B, S = x.shape[0], x.shape[1]
    shp = (B, S, N_GROUPS, N_SLOTS, HEAD_DIM)
    q = q.reshape(shp) * (HEAD_DIM ** -0.5)
    return u, z_a, q, k.reshape(shp), v.reshape(shp), z_b, g_a, g_b


def depthwise_valid(u_hist, conv_w):
    return lax.conv_general_dilated(u_hist, conv_w[:, None, :], window_strides=(1,), padding='VALID',
                                    dimension_numbers=('NWC', 'WIO', 'NWC'), feature_group_count=CONV_CH)


def dilated_prompt(q, k, v, window, dil, slopes):
    B, S, H, Dh = q.shape
    steps = window // dil
    span = dil * Q_BLK
    s_pad = -(-S // span) * span
    nb = s_pad // span
    pad = ((0, 0), (0, s_pad - S), (0, 0), (0, 0))
    blk = lambda t: jnp.pad(t, pad).reshape(B, nb, Q_BLK, dil, H, Dh)
    qb, kb, vb = blk(q), blk(k), blk(v)

    def with_prev(t):
        prev = jnp.pad(t[:, :-1], ((0, 0), (1, 0), (0, 0), (0, 0), (0, 0), (0, 0)))
        return jnp.concatenate([prev, t], axis=2)

    kk, vv = with_prev(kb), with_prev(vb)
    s = jnp.einsum('bnirhd,bnjrhd->bnrhij', qb, kk, preferred_element_type=jnp.float32)
    i = jnp.arange(Q_BLK)[:, None]
    j = jnp.arange(2 * Q_BLK)[None, :]
    delta = i + Q_BLK - j
    n = jnp.arange(nb)[:, None, None]
    valid = (delta >= 0) & (delta <= steps) & ((n > 0) | (j >= Q_BLK))
    bias = -slopes[:, None, None] * (delta * dil).astype(jnp.float32)
    s = jnp.where(valid[None, :, None, None], s + bias, -jnp.inf)
    lse = jax.nn.logsumexp(s, axis=-1)
    p = jnp.exp(s - lse[..., None]).astype(v.dtype)
    o = jnp.einsum('bnrhij,bnjrhd->bnirhd', p, vv)
    o = o.reshape(B, s_pad, H, Dh)[:, :S]
    lse = lse.transpose(0, 1, 4, 2, 3).reshape(B, s_pad, H)[:, :S]
    return o, lse


def dilated_sample(q, kv_buf, k, v, window, dil, slopes):
    T = q.shape[1]
    wb = kv_buf.shape[1]
    steps = window // dil
    kcat = jnp.concatenate([kv_buf[:, :, 0], k], axis=1)
    vcat = jnp.concatenate([kv_buf[:, :, 1], v], axis=1)
    jstep = jnp.arange(steps + 1)
    idx = wb + jnp.arange(T)[:, None] - jstep[None, :] * dil
    valid = idx >= 0
    idxc = jnp.maximum(idx, 0)
    kg = kcat[:, idxc]
    vg = vcat[:, idxc]
    s = jnp.einsum('bthd,btkhd->bhtk', q, kg, preferred_element_type=jnp.float32)
    s = s - slopes[:, None, None] * (jstep * dil).astype(jnp.float32)[None, None, :]
    s = jnp.where(valid, s, -jnp.inf)
    lse = jax.nn.logsumexp(s, axis=-1)
    p = jnp.exp(s - lse[..., None]).astype(v.dtype)
    o = jnp.einsum('bhtk,btkhd->bthd', p, vg)
    new_buf = jnp.stack([kcat, vcat], axis=2)[:, -wb:]
    return o, lse.transpose(0, 2, 1), new_buf


def combine_groups(outs, lses):
    o = jnp.stack(outs, axis=2)
    w = jax.nn.softmax(jnp.stack(lses, axis=2), axis=2)
    o = jnp.sum(w[..., None].astype(o.dtype) * o, axis=2)
    return o.reshape(o.shape[0], o.shape[1], ATT_W)


def merge_and_norm(x, conv_raw, attn, z_a, z_b, g_a, g_b, conv_b, conv_ln_g, conv_ln_b,
                   w_a, w_b, w_out, ln_g, ln_b):
    c = jax.nn.silu(layer_norm(conv_raw + conv_b, conv_ln_g, conv_ln_b))
    pa = jnp.einsum('bsc,cd->bsd', c * jax.nn.silu(z_a), w_a)
    pb = jnp.einsum('bsc,cd->bsd', attn * jax.nn.silu(z_b), w_b)
    m = jax.nn.sigmoid(g_a) * pa + jax.nn.sigmoid(g_b) * pb
    o = jnp.einsum('bsd,de->bse', m, w_out)
    return layer_norm(ALPHA * x + o, ln_g, ln_b)


def setup_inputs(seed: int = 0) -> dict:
    key = jax.random.key(seed)
    ks = jax.random.split(key, 20)
    f32 = jnp.float32
    nrm = lambda k, shp: jax.random.normal(k, shp, f32)
    x_prompt = nrm(ks[0], (BATCH, SEQ, D_MODEL))
    x_sample = nrm(ks[1], (DEC_BATCH, DEC_SEQ, D_MODEL))
    caches = []
    for g, (win, dil) in enumerate(GROUPS):
        wb = min(win, PAST_LEN)
        caches.append(nrm(ks[2 + g], (DEC_BATCH, wb, 2, N_SLOTS, HEAD_DIM)))
    state_conv = 0.5 * nrm(ks[5], (DEC_BATCH, CONV_WIDTH - 1, CONV_CH))
    col_scale = np.ones((IN_W,), np.float32)
    v_start = 3 * CONV_CH + 2 * QKV_W
    col_scale[v_start:v_start + QKV_W] = BETA
    w_in = nrm(ks[6], (D_MODEL, IN_W)) * (D_MODEL ** -0.5) * jnp.asarray(col_scale)
    b_in = 0.01 * nrm(ks[7], (IN_W,))
    conv_w = nrm(ks[8], (CONV_WIDTH, CONV_CH)) * (CONV_WIDTH ** -0.5)
    conv_b = 0.01 * nrm(ks[9], (CONV_CH,))
    conv_ln_g = 1.0 + 0.01 * nrm(ks[10], (CONV_CH,))
    conv_ln_b = 0.01 * nrm(ks[11], (CONV_CH,))
    w_a = nrm(ks[12], (CONV_CH, D_MODEL)) * (CONV_CH ** -0.5) * BETA
    w_b = nrm(ks[13], (ATT_W, D_MODEL)) * (ATT_W ** -0.5) * BETA
    w_out = nrm(ks[14], (D_MODEL, D_MODEL)) * (D_MODEL ** -0.5) * BETA
    ln_g = 1.0 + 0.01 * nrm(ks[15], (D_MODEL,))
    ln_b = 0.01 * nrm(ks[16], (D_MODEL,))
    return {'x_prompt': x_prompt, 'x_sample': x_sample,
            'cache_kv_w128': caches[0], 'cache_kv_w512': caches[1], 'cache_kv_w2048': caches[2],
            'state_conv': state_conv,
            'w_in': w_in, 'b_in': b_in, 'conv_w': conv_w, 'conv_b': conv_b,
            'conv_ln_g': conv_ln_g, 'conv_ln_b': conv_ln_b,
            'w_a': w_a, 'w_b': w_b, 'w_out': w_out, 'ln_g': ln_g, 'ln_b': ln_b}


def reference(x_prompt, x_sample, cache_kv_w128, cache_kv_w512, cache_kv_w2048, state_conv,
              w_in, b_in, conv_w, conv_b, conv_ln_g, conv_ln_b, w_a, w_b, w_out, ln_g, ln_b):
    slopes = alibi_slopes()
    assert DEPTH == 1
    y_p, y_s = x_prompt, x_sample
    for _layer in range(DEPTH):
        S = y_p.shape[1]
        u, z_a, q, k, v, z_b, g_a, g_b = in_projection(y_p, w_in, b_in)
        u_hist = jnp.pad(u, ((0, 0), (CONV_WIDTH - 1, 0), (0, 0)))
        conv_raw = depthwise_valid(u_hist, conv_w)
        conv_p = u_hist[:, -(CONV_WIDTH - 1):]
        outs, lses, kv_p = [], [], []
        for g, (win, dil) in enumerate(GROUPS):
            o, l = dilated_prompt(q[:, :, g], k[:, :, g], v[:, :, g], win, dil, slopes)
            outs.append(o)
            lses.append(l)
            keep = min(win, S)
            kv_p.append(jnp.stack([k[:, S - keep:, g], v[:, S - keep:, g]], axis=2))
        attn = combine_groups(outs, lses)
        y_p = merge_and_norm(y_p, conv_raw, attn, z_a, z_b, g_a, g_b, conv_b, conv_ln_g, conv_ln_b,
                             w_a, w_b, w_out, ln_g, ln_b)

        u, z_a, q, k, v, z_b, g_a, g_b = in_projection(y_s, w_in, b_in)
        u_hist = jnp.concatenate([state_conv.astype(u.dtype), u], axis=1)
        conv_raw = depthwise_valid(u_hist, conv_w)
        conv_s = u_hist[:, -(CONV_WIDTH - 1):]
        bufs = (cache_kv_w128, cache_kv_w512, cache_kv_w2048)
        outs, lses, kv_s = [], [], []
        for g, (win, dil) in enumerate(GROUPS):
            o, l, nb_ = dilated_sample(q[:, :, g], bufs[g].astype(k.dtype), k[:, :, g], v[:, :, g], win, dil, slopes)
            outs.append(o)
            lses.append(l)
            kv_s.append(nb_)
        attn = combine_groups(outs, lses)
        y_s = merge_and_norm(y_s, conv_raw, attn, z_a, z_b, g_a, g_b, conv_b, conv_ln_g, conv_ln_b,
                             w_a, w_b, w_out, ln_g, ln_b)
    return (y_p, y_s, kv_p[0], kv_p[1], kv_p[2], conv_p, kv_s[0], kv_s[1], kv_s[2], conv_s)
```

```python
import functools

import numpy as np
import jax
import jax.numpy as jnp
from jax import lax
from jax.experimental import pallas as pl
from jax.experimental.pallas import tpu as pltpu

F32 = jnp.float32
BF16 = jnp.bfloat16

D_MODEL = 1024
CONV_CH = 1024
CONV_WIDTH = 31
N_SLOTS = 8
HEAD_DIM = 64
GROUPS = ((128, 1), (512, 4), (2048, 16))
N_GROUPS = len(GROUPS)
ATT_W = N_SLOTS * HEAD_DIM
QKV_W = N_GROUPS * ATT_W
Q_BLK = 128
DEPTH = 1
ALPHA = (2.0 * DEPTH) ** 0.25
LN_EPS = 1e-5
NEG = -1e30

O_AVAL, O_AGATE, O_ZA = 0, CONV_CH, 2 * CONV_CH
O_Q = 3 * CONV_CH
O_K = O_Q + QKV_W
O_V = O_K + QKV_W
O_ZB = O_V + QKV_W
O_GA = O_ZB + ATT_W
O_GB = O_GA + D_MODEL
IN_W = O_GB + D_MODEL

HIST = 32
HOFF = HIST - (CONV_WIDTH - 1)
LANES = 128
VMEM_LIMIT = 56 * 1024 * 1024


def _dot(a, b):
    return jnp.dot(a, b, preferred_element_type=F32)


def _dot_nt(a, b):
    return lax.dot_general(a, b, (((1,), (1,)), ((), ())), preferred_element_type=F32)


def _sigmoid(x):
    return 1.0 / (1.0 + jnp.exp(-x))


def _silu(x):
    return x * _sigmoid(x)


def _layer_norm(x, g, b):
    mu = jnp.mean(x, axis=-1, keepdims=True)
    xc = x - mu
    var = jnp.mean(xc * xc, axis=-1, keepdims=True)
    return xc * lax.rsqrt(var + LN_EPS) * g + b


def _const_spec(shape, index):
    return pl.BlockSpec(shape, lambda *_: index, pipeline_mode=pl.Buffered(1))


def _alibi_slopes():
    return (2.0 ** (-8.0 * np.arange(1, N_SLOTS + 1, dtype=np.float64) / N_SLOTS)).astype(np.float32)


def _conv_branch_kernel(x_ref, w3_ref, wga_ref, b3_ref, bga_ref, cw_ref, cb_ref, lg_ref, lb_ref, wa_ref,
                        gapa_ref, cst_ref, uh_ref, cv_ref, ca_ref, *, tm, ns):
    s = pl.program_id(1)
    ch = 256

    @pl.when(s == 0)
    def _():
        uh_ref[0:HIST, :] = jnp.zeros((HIST, CONV_CH), F32)

    xb = x_ref[...].astype(BF16)

    def proj(c0, n):
        return _dot(xb, w3_ref[:, c0:c0 + n]) + b3_ref[:, c0:c0 + n]

    for c in range(0, CONV_CH, ch):
        a = proj(O_AVAL + c, ch)
        gt = proj(O_AGATE + c, ch)
        uh_ref[HIST:HIST + tm, c:c + ch] = a * _sigmoid(gt)

    rc = 64

    def conv_rows(i, carry):
        base = pl.multiple_of(HIST - 8 + i * rc, 8)
        for c in range(0, CONV_CH, ch):
            run = None
            for b in range(7, -1, -1):
                part = None
                for a in range((CONV_WIDTH - 1 - b) // 8 + 1):
                    j = CONV_WIDTH - 1 - (8 * a + b)
                    wj = jnp.concatenate([cw_ref[j * 8:(j + 1) * 8, c:c + ch]] * (rc // 8 + 1), axis=0)
                    term = uh_ref[pl.ds(base - 8 * a, rc + 8), c:c + ch] * wj
                    part = term if part is None else part + term
                run = part if run is None else part + pltpu.roll(run, 1, axis=0)
            cv_ref[pl.ds(pl.multiple_of(i * rc, rc), rc), c:c + ch] = run[8:, :] + cb_ref[:, c:c + ch]
        return carry

    lax.fori_loop(0, tm // rc, conv_rows, 0)

    @pl.when(s == ns - 1)
    def _():
        cst_ref[...] = uh_ref[tm:tm + HIST, :]

    uh_ref[0:HIST, :] = uh_ref[tm:tm + HIST, :]

    x = cv_ref[...]
    mu = jnp.mean(x, axis=-1, keepdims=True)
    xc = x - mu
    var = jnp.mean(xc * xc, axis=-1, keepdims=True)
    rstd = lax.rsqrt(var + LN_EPS)
    for c in range(0, CONV_CH, ch):
        y = (cv_ref[:, c:c + ch] - mu) * rstd * lg_ref[:, c:c + ch] + lb_ref[:, c:c + ch]
        za = proj(O_ZA + c, ch)
        ca_ref[:, c:c + ch] = (_silu(y) * _silu(za)).astype(BF16)

    ca = ca_ref[...]
    for c in range(0, D_MODEL, ch):
        pa = _dot(ca, wa_ref[:, c:c + ch])
        ga = _dot(xb, wga_ref[:, c:c + ch]) + bga_ref[:, c:c + ch]
        gapa_ref[:, c:c + ch] = (_sigmoid(ga) * pa).astype(BF16)


def _conv_branch_prompt(x, wbf, bsc, cw8, conv_b, lg, lb, wa_bf, tm):
    B, S, _ = x.shape
    ns = S // tm
    kern = functools.partial(_conv_branch_kernel, tm=tm, ns=ns)
    return pl.pallas_call(
        kern,
        grid=(B, ns),
        in_specs=[
            pl.BlockSpec((None, tm, D_MODEL), lambda b, s: (b, s, 0)),
            _const_spec((D_MODEL, 3 * CONV_CH), (0, 0)),
            _const_spec((D_MODEL, D_MODEL), (0, O_GA // D_MODEL)),
            _const_spec((1, 3 * CONV_CH), (0, 0)),
            _const_spec((1, D_MODEL), (0, O_GA // D_MODEL)),
            _const_spec((CONV_WIDTH * 8, CONV_CH), (0, 0)),
            _const_spec((1, CONV_CH), (0, 0)),
            _const_spec((1, CONV_CH), (0, 0)),
            _const_spec((1, CONV_CH), (0, 0)),
            _const_spec((CONV_CH, D_MODEL), (0, 0)),
        ],
        out_specs=[
            pl.BlockSpec((None, tm, D_MODEL), lambda b, s: (b, s, 0)),
            pl.BlockSpec((None, HIST, CONV_CH), lambda b, s: (b, 0, 0)),
        ],
        out_shape=[
            jax.ShapeDtypeStruct((B, S, D_MODEL), BF16),
            jax.ShapeDtypeStruct((B, HIST, CONV_CH), F32),
        ],
        scratch_shapes=[
            pltpu.VMEM((HIST + tm, CONV_CH), F32),
            pltpu.VMEM((tm, CONV_CH), F32),
            pltpu.VMEM((tm, CONV_CH), BF16),
        ],
        compiler_params=pltpu.CompilerParams(
            dimension_semantics=("arbitrary", "arbitrary"), vmem_limit_bytes=VMEM_LIMIT),
        name="conv_branch_prompt",
    )(x, wbf, wbf, bsc, bsc, cw8, conv_b, lg, lb, wa_bf)


def _qkv_kernel(x_ref, wq_ref, wk_ref, wv_ref, wzb_ref, wgb_ref, bq_ref, bk_ref, bv_ref, bzb_ref, bgb_ref,
                q_ref, k_ref, v_ref, zb_ref, sgb_ref, kv1_ref, kv2_ref, kv3_ref, *, tm, ns, keeps):
    s = pl.program_id(1)
    xb = x_ref[...].astype(BF16)
    kv_refs = (kv1_ref, kv2_ref, kv3_ref)
    for g in range(N_GROUPS):
        c0 = g * ATT_W
        q_ref[:, c0:c0 + ATT_W] = (_dot(xb, wq_ref[:, c0:c0 + ATT_W]) + bq_ref[:, c0:c0 + ATT_W]).astype(BF16)
        kk = _dot(xb, wk_ref[:, c0:c0 + ATT_W]) + bk_ref[:, c0:c0 + ATT_W]
        vv = _dot(xb, wv_ref[:, c0:c0 + ATT_W]) + bv_ref[:, c0:c0 + ATT_W]
        k_ref[:, c0:c0 + ATT_W] = kk.astype(BF16)
        v_ref[:, c0:c0 + ATT_W] = vv.astype(BF16)
        rows = min(keeps[g], tm)
        nwt = keeps[g] // rows
        kvr = kv_refs[g]

        @pl.when(s >= ns - nwt)
        def _(kk=kk, vv=vv, kvr=kvr, rows=rows):
            kvr[:, 0:ATT_W] = kk[tm - rows:, :]
            kvr[:, ATT_W:2 * ATT_W] = vv[tm - rows:, :]

    zb = _dot(xb, wzb_ref[...]) + bzb_ref[...]
    zb_ref[...] = _silu(zb).astype(BF16)
    for c in range(0, D_MODEL, 512):
        gb = _dot(xb, wgb_ref[:, c:c + 512]) + bgb_ref[:, c:c + 512]
        sgb_ref[:, c:c + 512] = _sigmoid(gb).astype(BF16)


def _qkv_prompt(x, wbf, bsc, tm):
    B, S, _ = x.shape
    ns = S // tm
    keeps = tuple(min(w, S) for w, _ in GROUPS)
    kern = functools.partial(_qkv_kernel, tm=tm, ns=ns, keeps=keeps)
    tok = lambda width: pl.BlockSpec((None, tm, width), lambda b, s: (b, s, 0))

    def kv_spec(keep):
        rows = min(keep, tm)
        first = ns - keep // rows
        return pl.BlockSpec((None, rows, 2 * ATT_W), lambda b, s: (b, jnp.maximum(s - first, 0), 0))

    wspec = lambda width, off: _const_spec((D_MODEL, width), (0, off // width))
    bspec = lambda width, off: _const_spec((1, width), (0, off // width))
    return pl.pallas_call(
        kern,
        grid=(B, ns),
        in_specs=[
            tok(D_MODEL),
            wspec(QKV_W, O_Q), wspec(QKV_W, O_K), wspec(QKV_W, O_V), wspec(ATT_W, O_ZB), wspec(D_MODEL, O_GB),
            bspec(QKV_W, O_Q), bspec(QKV_W, O_K), bspec(QKV_W, O_V), bspec(ATT_W, O_ZB), bspec(D_MODEL, O_GB),
        ],
        out_specs=[tok(QKV_W), tok(QKV_W), tok(QKV_W), tok(ATT_W), tok(D_MODEL)] + [kv_spec(k) for k in keeps],
        out_shape=[
            jax.ShapeDtypeStruct((B, S, QKV_W), BF16),
            jax.ShapeDtypeStruct((B, S, QKV_W), BF16),
            jax.ShapeDtypeStruct((B, S, QKV_W), BF16),
            jax.ShapeDtypeStruct((B, S, ATT_W), BF16),
            jax.ShapeDtypeStruct((B, S, D_MODEL), BF16),
        ] + [jax.ShapeDtypeStruct((B, k, 2 * ATT_W), F32) for k in keeps],
        compiler_params=pltpu.CompilerParams(
            dimension_semantics=("arbitrary", "arbitrary"), vmem_limit_bytes=VMEM_LIMIT),
        name="qkv_prompt",
    )(x, wbf, wbf, wbf, wbf, wbf, bsc, bsc, bsc, bsc, bsc)


def _prompt_bias_tables(dil):
    slopes = _alibi_slopes()
    i = np.arange(Q_BLK)[:, None]
    j = np.arange(2 * Q_BLK)[None, :]
    delta = i + Q_BLK - j
    band = (delta >= 0) & (delta <= Q_BLK)
    bias = -slopes[:, None, None] * (delta * dil).astype(np.float32)[None]
    rest = np.where(band[None], bias, np.float32(NEG)).astype(np.float32)
    first = np.where((band & (j >= Q_BLK))[None], bias, np.float32(NEG)).astype(np.float32)
    return np.stack([first, rest]), rest


def _attn_kernel(q_ref, kp_ref, kc_ref, vp_ref, vc_ref, b0_ref, br_ref, o_ref, lse_ref, *, nsub):
    lane = lax.broadcasted_iota(jnp.int32, (Q_BLK, LANES), 1)
    lo = lane < HEAD_DIM
    m_lo = lo.astype(F32).astype(BF16)
    m_hi = (1.0 - lo.astype(F32)).astype(BF16)
    for i in range(nsub):
        r0 = i * Q_BLK
        lse_tile = jnp.zeros((Q_BLK, LANES), F32)
        for hp in range(N_SLOTS // 2):
            l0 = hp * LANES
            q2 = q_ref[r0:r0 + Q_BLK, l0:l0 + LANES]
            if i == 0:
                k2 = jnp.concatenate([kp_ref[:, l0:l0 + LANES], kc_ref[0:Q_BLK, l0:l0 + LANES]], axis=0)
                v2 = jnp.concatenate([vp_ref[:, l0:l0 + LANES], vc_ref[0:Q_BLK, l0:l0 + LANES]], axis=0)
                bref = b0_ref
            else:
                k2 = kc_ref[r0 - Q_BLK:r0 + Q_BLK, l0:l0 + LANES]
                v2 = vc_ref[r0 - Q_BLK:r0 + Q_BLK, l0:l0 + LANES]
                bref = br_ref
            outs = []
            for par in range(2):
                h = 2 * hp + par
                qm = q2 * (m_lo if par == 0 else m_hi)
                sc = _dot_nt(qm, k2) + bref[h]
                mx = jnp.max(sc, axis=1, keepdims=True)
                p = jnp.exp(sc - mx)
                den = jnp.sum(p, axis=1, keepdims=True)
                o = _dot(p.astype(BF16), v2)
                outs.append(o / den)
                lse_tile = jnp.where(lane == h, mx + jnp.log(den), lse_tile)
            o_ref[r0:r0 + Q_BLK, l0:l0 + LANES] = jnp.where(lo, outs[0], outs[1]).astype(BF16)
        lse_ref[r0:r0 + Q_BLK, :] = lse_tile


def _attn_prompt(q, k, v, g):
    B, S, _ = q.shape
    _, dil = GROUPS[g]
    L = S // dil
    qb = min(512, L)
    nq = L // qb
    nsub = qb // Q_BLK
    b0, br = _prompt_bias_tables(dil)
    qv, kv, vv = (t.reshape(B, L, dil * QKV_W) for t in (q, k, v))
    cur = pl.BlockSpec((None, qb, ATT_W), lambda b, r, n: (b, n, r * N_GROUPS + g))
    prev = pl.BlockSpec((None, Q_BLK, ATT_W), lambda b, r, n: (b, jnp.maximum(n * nsub - 1, 0), r * N_GROUPS + g))
    o, lse = pl.pallas_call(
        functools.partial(_attn_kernel, nsub=nsub),
        grid=(B, dil, nq),
        in_specs=[
            cur, prev, cur, prev, cur,
            pl.BlockSpec((None, N_SLOTS, Q_BLK, 2 * Q_BLK), lambda b, r, n: (jnp.minimum(n, 1), 0, 0, 0)),
            _const_spec((N_SLOTS, Q_BLK, 2 * Q_BLK), (0, 0, 0)),
        ],
        out_specs=[
            pl.BlockSpec((None, qb, ATT_W), lambda b, r, n: (b, n, r)),
            pl.BlockSpec((None, qb, LANES), lambda b, r, n: (b, n, r)),
        ],
        out_shape=[
            jax.ShapeDtypeStruct((B, L, dil * ATT_W), BF16),
            jax.ShapeDtypeStruct((B, L, dil * LANES), F32),
        ],
        compiler_params=pltpu.CompilerParams(
            dimension_semantics=("arbitrary", "arbitrary", "arbitrary"), vmem_limit_bytes=VMEM_LIMIT),
        name=f"attn_prompt_g{g}",
    )(qv, kv, kv, vv, vv, jnp.asarray(b0), jnp.asarray(br))
    return o.reshape(B * S, ATT_W), lse.reshape(B * S, LANES)


def _merge_tail(x_ref, gapa_ref, sgb_ref, zb_ref, attn, wb_ref, wo_ref, g_ref, bt_ref, y_ref):
    ab = (attn * zb_ref[...].astype(F32)).astype(BF16)
    pb = _dot(ab, wb_ref[...])
    m = gapa_ref[...].astype(F32) + sgb_ref[...].astype(F32) * pb
    o = _dot(m.astype(BF16), wo_ref[...])
    y_ref[...] = _layer_norm(ALPHA * x_ref[...] + o, g_ref[...], bt_ref[...])


def _merge_groups_kernel(x_ref, gapa_ref, sgb_ref, zb_ref, o1_ref, o2_ref, o3_ref, l1_ref, l2_ref, l3_ref,
                         ex_ref, wb_ref, wo_ref, g_ref, bt_ref, y_ref):
    l1, l2, l3 = l1_ref[...], l2_ref[...], l3_ref[...]
    mx = jnp.maximum(jnp.maximum(l1, l2), l3)
    e1, e2, e3 = jnp.exp(l1 - mx), jnp.exp(l2 - mx), jnp.exp(l3 - mx)
    inv = 1.0 / (e1 + e2 + e3)
    ex = ex_ref[...]
    attn = (_dot((e1 * inv).astype(BF16), ex) * o1_ref[...].astype(F32)
            + _dot((e2 * inv).astype(BF16), ex) * o2_ref[...].astype(F32)
            + _dot((e3 * inv).astype(BF16), ex) * o3_ref[...].astype(F32))
    _merge_tail(x_ref, gapa_ref, sgb_ref, zb_ref, attn, wb_ref, wo_ref, g_ref, bt_ref, y_ref)


def _merge_plain_kernel(x_ref, gapa_ref, sgb_ref, zb_ref, attn_ref, wb_ref, wo_ref, g_ref, bt_ref, y_ref):
    _merge_tail(x_ref, gapa_ref, sgb_ref, zb_ref, attn_ref[...], wb_ref, wo_ref, g_ref, bt_ref, y_ref)


def _head_expand_matrix():
    ex = np.zeros((LANES, ATT_W), np.float32)
    for h in range(N_SLOTS):
        ex[h, h * HEAD_DIM:(h + 1) * HEAD_DIM] = 1.0
    return ex


def _merge(x2, gapa, sgb, zb, attn_parts, wb_bf, wo_bf, ln_g, ln_b, tm):
    T = x2.shape[0]
    tok = lambda width: pl.BlockSpec((tm, width), lambda t: (t, 0))
    weights = [_const_spec((ATT_W, D_MODEL), (0, 0)), _const_spec((D_MODEL, D_MODEL), (0, 0)),
               _const_spec((1, D_MODEL), (0, 0)), _const_spec((1, D_MODEL), (0, 0))]
    if len(attn_parts) == 1:
        kern = _merge_plain_kernel
        mid_specs, mid_args = [tok(ATT_W)], list(attn_parts)
    else:
        kern = _merge_groups_kernel
        os_, ls_ = zip(*attn_parts)
        mid_specs = [tok(ATT_W)] * 3 + [tok(LANES)] * 3 + [_const_spec((LANES, ATT_W), (0, 0))]
        mid_args = list(os_) + list(ls_) + [jnp.asarray(_head_expand_matrix(), BF16)]
    return pl.pallas_call(
        kern,
        grid=(T // tm,),
        in_specs=[tok(D_MODEL), tok(D_MODEL), tok(D_MODEL), tok(ATT_W)] + mid_specs + weights,
        out_specs=tok(D_MODEL),
        out_shape=jax.ShapeDtypeStruct((T, D_MODEL), F32),
        compiler_params=pltpu.CompilerParams(
            dimension_semantics=("arbitrary",), vmem_limit_bytes=VMEM_LIMIT),
        name="merge_groups" if len(attn_parts) > 1 else "merge_plain",
    )(x2, gapa, sgb, zb, *mid_args, wb_bf, wo_bf, ln_g, ln_b)


def _sample_proj_kernel(x_ref, w_ref, b_ref, h_ref):
    h_ref[...] = _dot(x_ref[...].astype(BF16), w_ref[...]) + b_ref[...]


def _sample_proj(xs2, wbf, bsc):
    T = xs2.shape[0]
    cw = 1024
    return pl.pallas_call(
        _sample_proj_kernel,
        grid=(IN_W // cw,),
        in_specs=[
            pl.BlockSpec((T, D_MODEL), lambda j: (0, 0)),
            pl.BlockSpec((D_MODEL, cw), lambda j: (0, j)),
            pl.BlockSpec((1, cw), lambda j: (0, j)),
        ],
        out_specs=pl.BlockSpec((T, cw), lambda j: (0, j)),
        out_shape=jax.ShapeDtypeStruct((T, IN_W), F32),
        compiler_params=pltpu.CompilerParams(dimension_semantics=("arbitrary",), vmem_limit_bytes=VMEM_LIMIT),
        name="sample_proj",
    )(xs2, wbf, bsc)


def _sample_conv_kernel(h3_ref, hga_ref, hzb_ref, hgb_ref, st_ref, cw_ref, cb_ref, lg_ref, lb_ref, wa_ref,
                        gapa_ref, zb_ref, sgb_ref, cst_ref, uh_ref, cv_ref, *, nb, tq):
    a = h3_ref[:, O_AVAL:O_AVAL + CONV_CH]
    gt = h3_ref[:, O_AGATE:O_AGATE + CONV_CH]
    u = a * _sigmoid(gt)
    uh_ref[:, 0:HIST, :] = st_ref[...]
    uh_ref[:, HIST:HIST + tq, :] = u.reshape(nb, tq, CONV_CH)

    def conv_seq(b, carry):
        acc = jnp.zeros((tq, CONV_CH), F32)
        for j in range(CONV_WIDTH):
            acc = acc + uh_ref[b, pl.ds(HOFF + j, tq), :] * cw_ref[j * 8:(j + 1) * 8, :]
        cv_ref[pl.ds(pl.multiple_of(b * tq, tq), tq), :] = acc + cb_ref[...]
        return carry

    lax.fori_loop(0, nb, conv_seq, 0)
    cst_ref[...] = uh_ref[:, tq:tq + HIST, :]

    y = _layer_norm(cv_ref[...], lg_ref[...], lb_ref[...])
    ca = (_silu(y) * _silu(h3_ref[:, O_ZA:O_ZA + CONV_CH])).astype(BF16)
    gapa_ref[...] = (_sigmoid(hga_ref[...]) * _dot(ca, wa_ref[...])).astype(BF16)
    zb_ref[...] = _silu(hzb_ref[...]).astype(BF16)
    sgb_ref[...] = _sigmoid(hgb_ref[...]).astype(BF16)


def _sample_conv_branch(hs, state_pad, cw8, conv_b, lg, lb, wa_bf, nb, tq):
    T = nb * tq
    col = lambda width, off: pl.BlockSpec((T, width), lambda i: (0, off // width))
    full = lambda shape: pl.BlockSpec(shape, lambda i: (0,) * len(shape))
    return pl.pallas_call(
        functools.partial(_sample_conv_kernel, nb=nb, tq=tq),
        grid=(1,),
        in_specs=[
            col(3 * CONV_CH, 0), col(D_MODEL, O_GA), col(ATT_W, O_ZB), col(D_MODEL, O_GB),
            full((nb, HIST, CONV_CH)), full((CONV_WIDTH * 8, CONV_CH)),
            full((1, CONV_CH)), full((1, CONV_CH)), full((1, CONV_CH)), full((CONV_CH, D_MODEL)),
        ],
        out_specs=[full((T, D_MODEL)), full((T, ATT_W)), full((T, D_MODEL)), full((nb, HIST, CONV_CH))],
        out_shape=[
            jax.ShapeDtypeStruct((T, D_MODEL), BF16),
            jax.ShapeDtypeStruct((T, ATT_W), BF16),
            jax.ShapeDtypeStruct((T, D_MODEL), BF16),
            jax.ShapeDtypeStruct((nb, HIST, CONV_CH), F32),
        ],
        scratch_shapes=[pltpu.VMEM((nb, HIST + tq, CONV_CH), F32), pltpu.VMEM((T, CONV_CH), F32)],
        compiler_params=pltpu.CompilerParams(dimension_semantics=("arbitrary",), vmem_limit_bytes=VMEM_LIMIT),
        name="sample_conv_branch",
    )(hs, hs, hs, hs, state_pad, cw8, conv_b, lg, lb, wa_bf)


def _sample_bias_tables(g, tq):
    _, dil = GROUPS[g]
    nres = min(dil, tq)
    slopes = _alibi_slopes()
    tabc = np.full((nres, N_SLOTS * tq, Q_BLK), NEG, np.float32)
    tabn = np.full((N_SLOTS * tq, tq), NEG, np.float32)
    for h in range(N_SLOTS):
        for t in range(tq):
            row = h * tq + t
            for rho in range(nres):
                if t >= rho and (t - rho) % dil == 0:
                    a = (t - rho) // dil
                    i = np.arange(a, Q_BLK)
                    tabc[rho, row, i] = -slopes[h] * ((Q_BLK + a - i) * dil).astype(np.float32)
            for t2 in range(t + 1):
                if (t - t2) % dil == 0:
                    tabn[row, t2] = -slopes[h] * np.float32(t - t2)
    return tabc, tabn


def _sample_attn_kernel(q_ref, k_ref, v_ref, c1_ref, c2_ref, c3_ref, tc1_ref, tc2_ref, tc3_ref,
                        tn1_ref, tn2_ref, tn3_ref, hm_ref, o_ref, *, tq):
    caches = (c1_ref, c2_ref, c3_ref)
    tabc = (tc1_ref, tc2_ref, tc3_ref)
    tabn = (tn1_ref, tn2_ref, tn3_ref)
    hm = hm_ref[...]
    row = 2 * ATT_W
    scores, mx = [], None
    for g in range(N_GROUPS):
        _, dil = GROUPS[g]
        nres = min(dil, tq)
        c0 = g * ATT_W
        qg = q_ref[:, c0:c0 + ATT_W]
        qbd = (jnp.concatenate([qg] * N_SLOTS, axis=0) * hm).astype(BF16)
        per = []
        for rho in range(nres):
            kc = caches[g][:, rho * row:rho * row + ATT_W].astype(BF16)
            sc = _dot_nt(qbd, kc) + tabc[g][rho]
            per.append(sc)
            m = jnp.max(sc, axis=1, keepdims=True)
            mx = m if mx is None else jnp.maximum(mx, m)
        sn = _dot_nt(qbd, k_ref[:, c0:c0 + ATT_W].astype(BF16)) + tabn[g][...]
        mx = jnp.maximum(mx, jnp.max(sn, axis=1, keepdims=True))
        scores.append((per, sn))
    acc = jnp.zeros((N_SLOTS * tq, ATT_W), F32)
    den = jnp.zeros((N_SLOTS * tq, 1), F32)
    for g in range(N_GROUPS):
        c0 = g * ATT_W
        per, sn = scores[g]
        for rho, sc in enumerate(per):
            p = jnp.exp(sc - mx)
            den = den + jnp.sum(p, axis=1, keepdims=True)
            vc = caches[g][:, rho * row + ATT_W:(rho + 1) * row].astype(BF16)
            acc = acc + _dot(p.astype(BF16), vc)
        pn = jnp.exp(sn - mx)
        den = den + jnp.sum(pn, axis=1, keepdims=True)
        acc = acc + _dot(pn.astype(BF16), v_ref[:, c0:c0 + ATT_W].astype(BF16))
    out = (acc / den) * hm
    o_ref[...] = jnp.sum(out.reshape(N_SLOTS, tq, ATT_W), axis=0)


def _sample_attn(hs3, caches, nb, tq):
    row = 2 * ATT_W
    views, cspecs, tcs, tns = [], [], [], []
    for g, (win, dil) in enumerate(GROUPS):
        nres = min(dil, tq)
        views.append(caches[g].reshape(nb, win // dil, dil * row))
        cspecs.append(pl.BlockSpec((None, win // dil, nres * row), lambda b: (b, 0, 0)))
        tc, tn = _sample_bias_tables(g, tq)
        tcs.append(jnp.asarray(tc))
        tns.append(jnp.asarray(tn))
    hm = np.zeros((N_SLOTS * tq, ATT_W), np.float32)
    for h in range(N_SLOTS):
        hm[h * tq:(h + 1) * tq, h * HEAD_DIM:(h + 1) * HEAD_DIM] = 1.0
    qkv = lambda off: pl.BlockSpec((None, tq, QKV_W), lambda b: (b, 0, off // QKV_W))
    const = lambda a: _const_spec(a.shape, (0,) * a.ndim)
    return pl.pallas_call(
        functools.partial(_sample_attn_kernel, tq=tq),
        grid=(nb,),
        in_specs=[qkv(O_Q), qkv(O_K), qkv(O_V)] + cspecs + [const(t) for t in tcs] + [const(t) for t in tns]
        + [_const_spec(hm.shape, (0, 0))],
        out_specs=pl.BlockSpec((None, tq, ATT_W), lambda b: (b, 0, 0)),
        out_shape=jax.ShapeDtypeStruct((nb, tq, ATT_W), F32),
        compiler_params=pltpu.CompilerParams(dimension_semantics=("arbitrary",), vmem_limit_bytes=VMEM_LIMIT),
        name="sample_attn",
    )(hs3, hs3, hs3, *views, *tcs, *tns, jnp.asarray(hm))


def kernel(x_prompt, x_sample, cache_kv_w128, cache_kv_w512, cache_kv_w2048, state_conv, w_in, b_in, conv_w, conv_b,
           conv_ln_g, conv_ln_b, w_a, w_b, w_out, ln_g, ln_b):
    B, S, _ = x_prompt.shape
    nb, tq, _ = x_sample.shape
    caches = (cache_kv_w128, cache_kv_w512, cache_kv_w2048)
    for (win, dil), c in zip(GROUPS, caches):
        assert c.shape[1] == win and S % (dil * Q_BLK) == 0 and S >= win
    assert tq == 8 and state_conv.shape[1] == CONV_WIDTH - 1

    col_scale = np.ones((IN_W,), np.float32)
    col_scale[O_Q:O_K] = HEAD_DIM ** -0.5
    wbf = (w_in * col_scale).astype(BF16)
    bsc = (b_in * col_scale).reshape(1, IN_W)
    cw8 = jnp.broadcast_to(conv_w[:, None, :], (CONV_WIDTH, 8, CONV_CH)).reshape(CONV_WIDTH * 8, CONV_CH)
    cb, clg, clb = (t.reshape(1, CONV_CH) for t in (conv_b, conv_ln_g, conv_ln_b))
    lg, lb = ln_g.reshape(1, D_MODEL), ln_b.reshape(1, D_MODEL)
    wa_bf, wb_bf, wo_bf = w_a.astype(BF16), w_b.astype(BF16), w_out.astype(BF16)

    tm = 512
    gapa, cst_p = _conv_branch_prompt(x_prompt, wbf, bsc, cw8, cb, clg, clb, wa_bf, tm)
    q, k, v, zb, sgb, kv1, kv2, kv3 = _qkv_prompt(x_prompt, wbf, bsc, tm)
    parts = [_attn_prompt(q, k, v, g) for g in range(N_GROUPS)]
    T = B * S
    y_p = _merge(x_prompt.reshape(T, D_MODEL), gapa.reshape(T, D_MODEL), sgb.reshape(T, D_MODEL),
                 zb.reshape(T, ATT_W), parts, wb_bf, wo_bf, lg, lb, tm).reshape(B, S, D_MODEL)
    kv_p = [t.reshape(B, t.shape[1], 2, N_SLOTS, HEAD_DIM) for t in (kv1, kv2, kv3)]
    conv_p = cst_p[:, HOFF:, :]

    Ts = nb * tq
    hs = _sample_proj(x_sample.reshape(Ts, D_MODEL), wbf, bsc)
    state_pad = jnp.pad(state_conv, ((0, 0), (HOFF, 0), (0, 0)))
    gapa_s, zb_s, sgb_s, cst_s = _sample_conv_branch(hs, state_pad, cw8, cb, clg, clb, wa_bf, nb, tq)
    hs3 = hs.reshape(nb, tq, IN_W)
    attn_s = _sample_attn(hs3, caches, nb, tq)
    y_s = _merge(x_sample.reshape(Ts, D_MODEL), gapa_s, sgb_s, zb_s, [attn_s.reshape(Ts, ATT_W)],
                 wb_bf, wo_bf, lg, lb, Ts).reshape(nb, tq, D_MODEL)
    kv_s = []
    for g, c in enumerate(caches):
        k_new = hs3[:, :, O_K + g * ATT_W:O_K + (g + 1) * ATT_W].reshape(nb, tq, 1, N_SLOTS, HEAD_DIM)
        v_new = hs3[:, :, O_V + g * ATT_W:O_V + (g + 1) * ATT_W].reshape(nb, tq, 1, N_SLOTS, HEAD_DIM)
        kv_s.append(jnp.concatenate([c[:, tq:], jnp.concatenate([k_new, v_new], axis=2)], axis=1))
    conv_s = cst_s[:, HOFF:, :]
    return (y_p, y_s, kv_p[0], kv_p[1], kv_p[2], conv_p, kv_s[0], kv_s[1], kv_s[2], conv_s)
```

```python
import functools

import numpy as np
import jax
import jax.numpy as jnp
from jax import lax
from jax.experimental import pallas as pl
from jax.experimental.pallas import tpu as pltpu

F32 = jnp.float32
BF16 = jnp.bfloat16

D_MODEL = 1024
CONV_CH = 1024
CONV_WIDTH = 31
N_SLOTS = 8
HEAD_DIM = 64
GROUPS = ((128, 1), (512, 4), (2048, 16))
N_GROUPS = len(GROUPS)
ATT_W = N_SLOTS * HEAD_DIM
QKV_W = N_GROUPS * ATT_W
Q_BLK = 128
DEPTH = 1
ALPHA = (2.0 * DEPTH) ** 0.25
LN_EPS = 1e-5
NEG = -1e30

O_AVAL, O_AGATE, O_ZA = 0, CONV_CH, 2 * CONV_CH
O_Q = 3 * CONV_CH
O_K = O_Q + QKV_W
O_V = O_K + QKV_W
O_ZB = O_V + QKV_W
O_GA = O_ZB + ATT_W
O_GB = O_GA + D_MODEL
IN_W = O_GB + D_MODEL

HIST = 32
HOFF = HIST - (CONV_WIDTH - 1)
LANES = 128
VMEM_LIMIT = 56 * 1024 * 1024


def _dot(a, b):
    return jnp.dot(a, b, preferred_element_type=F32)


def _dot_nt(a, b):
    return lax.dot_general(a, b, (((1,), (1,)), ((), ())), preferred_element_type=F32)


def _sigmoid(x):
    return 1.0 / (1.0 + jnp.exp(-x))


def _silu(x):
    return x * _sigmoid(x)


def _layer_norm(x, g, b):
    mu = jnp.mean(x, axis=-1, keepdims=True)
    xc = x - mu
    var = jnp.mean(xc * xc, axis=-1, keepdims=True)
    return xc * lax.rsqrt(var + LN_EPS) * g + b


def _const_spec(shape, index):
    return pl.BlockSpec(shape, lambda *_: index, pipeline_mode=pl.Buffered(1))


def _alibi_slopes():
    return (2.0 ** (-8.0 * np.arange(1, N_SLOTS + 1, dtype=np.float64) / N_SLOTS)).astype(np.float32)


def _conv_branch_kernel(x_ref, w3_ref, wga_ref, b3_ref, bga_ref, cw_ref, cb_ref, lg_ref, lb_ref, wa_ref,
                        gapa_ref, cst_ref, uh_ref, cv_ref, ca_ref, *, tm, ns):
    s = pl.program_id(1)
    ch = 256

    @pl.when(s == 0)
    def _():
        uh_ref[0:HIST, :] = jnp.zeros((HIST, CONV_CH), F32)

    xb = x_ref[...].astype(BF16)

    def proj(c0, n):
        return _dot(xb, w3_ref[:, c0:c0 + n]) + b3_ref[:, c0:c0 + n]

    for c in range(0, CONV_CH, ch):
        a = proj(O_AVAL + c, ch)
        gt = proj(O_AGATE + c, ch)
        uh_ref[HIST:HIST + tm, c:c + ch] = a * _sigmoid(gt)

    rc = 64

    def conv_rows(i, carry):
        base = pl.multiple_of(HIST - 8 + i * rc, 8)
        for c in range(0, CONV_CH, ch):
            run = None
            for b in range(7, -1, -1):
                part = None
                for a in range((CONV_WIDTH - 1 - b) // 8 + 1):
                    j = CONV_WIDTH - 1 - (8 * a + b)
                    wj = jnp.concatenate([cw_ref[j * 8:(j + 1) * 8, c:c + ch]] * (rc // 8 + 1), axis=0)
                    term = uh_ref[pl.ds(base - 8 * a, rc + 8), c:c + ch] * wj
                    part = term if part is None else part + term
                run = part if run is None else part + pltpu.roll(run, 1, axis=0)
            cv_ref[pl.ds(pl.multiple_of(i * rc, rc), rc), c:c + ch] = run[8:, :] + cb_ref[:, c:c + ch]
        return carry

    lax.fori_loop(0, tm // rc, conv_rows, 0)

    @pl.when(s == ns - 1)
    def _():
        cst_ref[...] = uh_ref[tm:tm + HIST, :]

    uh_ref[0:HIST, :] = uh_ref[tm:tm + HIST, :]

    x = cv_ref[...]
    mu = jnp.mean(x, axis=-1, keepdims=True)
    xc = x - mu
    var = jnp.mean(xc * xc, axis=-1, keepdims=True)
    rstd = lax.rsqrt(var + LN_EPS)
    for c in range(0, CONV_CH, ch):
        y = (cv_ref[:, c:c + ch] - mu) * rstd * lg_ref[:, c:c + ch] + lb_ref[:, c:c + ch]
        za = proj(O_ZA + c, ch)
        ca_ref[:, c:c + ch] = (_silu(y) * _silu(za)).astype(BF16)

    ca = ca_ref[...]
    for c in range(0, D_MODEL, ch):
        pa = _dot(ca, wa_ref[:, c:c + ch])
        ga = _dot(xb, wga_ref[:, c:c + ch]) + bga_ref[:, c:c + ch]
        gapa_ref[:, c:c + ch] = (_sigmoid(ga) * pa).astype(BF16)


def _conv_branch_prompt(x, wbf, bsc, cw8, conv_b, lg, lb, wa_bf, tm):
    B, S, _ = x.shape
    ns = S // tm
    kern = functools.partial(_conv_branch_kernel, tm=tm, ns=ns)
    return pl.pallas_call(
        kern,
        grid=(B, ns),
        in_specs=[
            pl.BlockSpec((None, tm, D_MODEL), lambda b, s: (b, s, 0)),
            _const_spec((D_MODEL, 3 * CONV_CH), (0, 0)),
            _const_spec((D_MODEL, D_MODEL), (0, O_GA // D_MODEL)),
            _const_spec((1, 3 * CONV_CH), (0, 0)),
            _const_spec((1, D_MODEL), (0, O_GA // D_MODEL)),
            _const_spec((CONV_WIDTH * 8, CONV_CH), (0, 0)),
            _const_spec((1, CONV_CH), (0, 0)),
            _const_spec((1, CONV_CH), (0, 0)),
            _const_spec((1, CONV_CH), (0, 0)),
            _const_spec((CONV_CH, D_MODEL), (0, 0)),
        ],
        out_specs=[
            pl.BlockSpec((None, tm, D_MODEL), lambda b, s: (b, s, 0)),
            pl.BlockSpec((None, HIST, CONV_CH), lambda b, s: (b, 0, 0)),
        ],
        out_shape=[
            jax.ShapeDtypeStruct((B, S, D_MODEL), BF16),
            jax.ShapeDtypeStruct((B, HIST, CONV_CH), F32),
        ],
        scratch_shapes=[
            pltpu.VMEM((HIST + tm, CONV_CH), F32),
            pltpu.VMEM((tm, CONV_CH), F32),
            pltpu.VMEM((tm, CONV_CH), BF16),
        ],
        compiler_params=pltpu.CompilerParams(
            dimension_semantics=("arbitrary", "arbitrary"), vmem_limit_bytes=VMEM_LIMIT),
        name="conv_branch_prompt",
    )(x, wbf, wbf, bsc, bsc, cw8, conv_b, lg, lb, wa_bf)


N_SLAB = D_MODEL // LANES


def _qkv_kernel(x_ref, wq_ref, wk_ref, wv_ref, wzb_ref, wgb_ref, bq_ref, bk_ref, bv_ref, bzb_ref, bgb_ref,
                o1_ref, o2_ref, o3_ref, zb_ref, sgb_ref, xs_ref, *, tm):
    xb = x_ref[...].astype(BF16)
    for sl in range(N_SLAB):
        xs_ref[sl] = x_ref[:, sl * LANES:(sl + 1) * LANES]
    outs = (o1_ref, o2_ref, o3_ref)
    for g, (_, dil) in enumerate(GROUPS):
        n = tm // dil
        if dil == 1:
            lhs = xb
        else:
            lhs = jnp.concatenate(
                [jnp.concatenate([xs_ref[sl, pl.ds(r, n, stride=dil), :] for sl in range(N_SLAB)], axis=1)
                 for r in range(dil)], axis=0).astype(BF16)
        c0 = g * ATT_W
        for part, (w_ref, b_ref) in enumerate(((wq_ref, bq_ref), (wk_ref, bk_ref), (wv_ref, bv_ref))):
            res = _dot(lhs, w_ref[:, c0:c0 + ATT_W]) + b_ref[:, c0:c0 + ATT_W]
            outs[g][:, :, part * ATT_W:(part + 1) * ATT_W] = res.astype(BF16).reshape(dil, n, ATT_W)
    zb = _dot(xb, wzb_ref[...]) + bzb_ref[...]
    zb_ref[...] = _silu(zb).astype(BF16)
    for c in range(0, D_MODEL, 512):
        gb = _dot(xb, wgb_ref[:, c:c + 512]) + bgb_ref[:, c:c + 512]
        sgb_ref[:, c:c + 512] = _sigmoid(gb).astype(BF16)


def _qkv_prompt(x, wbf, bsc, tm):
    B, S, _ = x.shape
    ns = S // tm
    tok = lambda width: pl.BlockSpec((None, tm, width), lambda b, s: (b, s, 0))
    grp = lambda dil: pl.BlockSpec((None, dil, tm // dil, QKV_W), lambda b, s: (b, 0, s, 0))
    wspec = lambda width, off: _const_spec((D_MODEL, width), (0, off // width))
    bspec = lambda width, off: _const_spec((1, width), (0, off // width))
    return pl.pallas_call(
        functools.partial(_qkv_kernel, tm=tm),
        grid=(B, ns),
        in_specs=[
            tok(D_MODEL),
            wspec(QKV_W, O_Q), wspec(QKV_W, O_K), wspec(QKV_W, O_V), wspec(ATT_W, O_ZB), wspec(D_MODEL, O_GB),
            bspec(QKV_W, O_Q), bspec(QKV_W, O_K), bspec(QKV_W, O_V), bspec(ATT_W, O_ZB), bspec(D_MODEL, O_GB),
        ],
        out_specs=[grp(d) for _, d in GROUPS] + [tok(ATT_W), tok(D_MODEL)],
        out_shape=[jax.ShapeDtypeStruct((B, d, S // d, QKV_W), BF16) for _, d in GROUPS] + [
            jax.ShapeDtypeStruct((B, S, ATT_W), BF16),
            jax.ShapeDtypeStruct((B, S, D_MODEL), BF16),
        ],
        scratch_shapes=[pltpu.VMEM((N_SLAB, tm, LANES), F32)],
        compiler_params=pltpu.CompilerParams(
            dimension_semantics=("arbitrary", "arbitrary"), vmem_limit_bytes=VMEM_LIMIT),
        name="qkv_prompt",
    )(x, wbf, wbf, wbf, wbf, wbf, bsc, bsc, bsc, bsc, bsc)


def _kv_window_kernel(x_ref, wkt_ref, wvt_ref, bkt_ref, bvt_ref, o1_ref, o2_ref, o3_ref, *, tw, nw, keeps):
    j = pl.program_id(1)
    xb = x_ref[...].astype(BF16)
    for g, o_ref in enumerate((o1_ref, o2_ref, o3_ref)):
        rows = min(keeps[g], tw)
        nwt = keeps[g] // rows
        c0 = g * ATT_W

        @pl.when(j >= nw - nwt)
        def _(o_ref=o_ref, rows=rows, c0=c0):
            kt = _dot_nt(wkt_ref[c0:c0 + ATT_W, :], xb) + bkt_ref[c0:c0 + ATT_W, :]
            vt = _dot_nt(wvt_ref[c0:c0 + ATT_W, :], xb) + bvt_ref[c0:c0 + ATT_W, :]
            o_ref[0] = kt[:, tw - rows:]
            o_ref[1] = vt[:, tw - rows:]


def _kv_window(x, wkt, wvt, bkt, bvt, tw):
    B, S, _ = x.shape
    keeps = tuple(min(w, S) for w, _ in GROUPS)
    nw = max(keeps) // tw
    first = S // tw - nw

    def out_spec(keep):
        rows = min(keep, tw)
        f = nw - keep // rows
        return pl.BlockSpec((None, 2, ATT_W, rows), lambda b, j: (b, 0, 0, jnp.maximum(j - f, 0)))

    return pl.pallas_call(
        functools.partial(_kv_window_kernel, tw=tw, nw=nw, keeps=keeps),
        grid=(B, nw),
        in_specs=[
            pl.BlockSpec((None, tw, D_MODEL), lambda b, j: (b, first + j, 0)),
            _const_spec((QKV_W, D_MODEL), (0, 0)), _const_spec((QKV_W, D_MODEL), (0, 0)),
            _const_spec((QKV_W, 1), (0, 0)), _const_spec((QKV_W, 1), (0, 0)),
        ],
        out_specs=[out_spec(k) for k in keeps],
        out_shape=[jax.ShapeDtypeStruct((B, 2, ATT_W, k), F32) for k in keeps],
        compiler_params=pltpu.CompilerParams(
            dimension_semantics=("arbitrary", "arbitrary"), vmem_limit_bytes=VMEM_LIMIT),
        name="kv_window",
    )(x, wkt, wvt, bkt, bvt)


def _prompt_bias_tables(dil):
    slopes = _alibi_slopes()
    i = np.arange(Q_BLK)[:, None]
    j = np.arange(2 * Q_BLK)[None, :]
    delta = i + Q_BLK - j
    band = (delta >= 0) & (delta <= Q_BLK)
    bias = -slopes[:, None, None] * (delta * dil).astype(np.float32)[None]
    rest = np.where(band[None], bias, np.float32(NEG)).astype(np.float32)
    first = np.where((band & (j >= Q_BLK))[None], bias, np.float32(NEG)).astype(np.float32)
    return np.stack([first, rest]), rest


def _attn_kernel(q_ref, kp_ref, kc_ref, vp_ref, vc_ref, b0_ref, br_ref, o_ref, lse_ref, *, nsub):
    lane = lax.broadcasted_iota(jnp.int32, (Q_BLK, LANES), 1)
    lo = lane < HEAD_DIM
    m_lo = lo.astype(F32).astype(BF16)
    m_hi = (1.0 - lo.astype(F32)).astype(BF16)
    for i in range(nsub):
        r0 = i * Q_BLK
        lse_tile = jnp.zeros((Q_BLK, LANES), F32)
        for hp in range(N_SLOTS // 2):
            l0 = hp * LANES
            q2 = q_ref[r0:r0 + Q_BLK, l0:l0 + LANES]
            if i == 0:
                k2 = jnp.concatenate([kp_ref[:, l0:l0 + LANES], kc_ref[0:Q_BLK, l0:l0 + LANES]], axis=0)
                v2 = jnp.concatenate([vp_ref[:, l0:l0 + LANES], vc_ref[0:Q_BLK, l0:l0 + LANES]], axis=0)
                bref = b0_ref
            else:
                k2 = kc_ref[r0 - Q_BLK:r0 + Q_BLK, l0:l0 + LANES]
                v2 = vc_ref[r0 - Q_BLK:r0 + Q_BLK, l0:l0 + LANES]
                bref = br_ref
            outs = []
            for par in range(2):
                h = 2 * hp + par
                qm = q2 * (m_lo if par == 0 else m_hi)
                sc = _dot_nt(qm, k2) + bref[h]
                mx = jnp.max(sc, axis=1, keepdims=True)
                p = jnp.exp(sc - mx)
                den = jnp.sum(p, axis=1, keepdims=True)
                o = _dot(p.astype(BF16), v2)
                outs.append(o / den)
                lse_tile = jnp.where(lane == h, mx + jnp.log(den), lse_tile)
            o_ref[r0:r0 + Q_BLK, l0:l0 + LANES] = jnp.where(lo, outs[0], outs[1]).astype(BF16)
        lse_ref[r0:r0 + Q_BLK, :] = lse_tile


def _attn_prompt(qkv, g):
    B, dil, L, _ = qkv.shape
    qb = min(512, L)
    nq = L // qb
    nsub = qb // Q_BLK
    b0, br = _prompt_bias_tables(dil)
    cur = lambda part: pl.BlockSpec((None, None, qb, ATT_W), lambda b, r, n: (b, r, n, part))
    prev = lambda part: pl.BlockSpec((None, None, Q_BLK, ATT_W),
                                     lambda b, r, n: (b, r, jnp.maximum(n * nsub - 1, 0), part))
    return pl.pallas_call(
        functools.partial(_attn_kernel, nsub=nsub),
        grid=(B, dil, nq),
        in_specs=[
            cur(0), prev(1), cur(1), prev(2), cur(2),
            pl.BlockSpec((None, N_SLOTS, Q_BLK, 2 * Q_BLK), lambda b, r, n: (jnp.minimum(n, 1), 0, 0, 0)),
            _const_spec((N_SLOTS, Q_BLK, 2 * Q_BLK), (0, 0, 0)),
        ],
        out_specs=[
            pl.BlockSpec((None, None, qb, ATT_W), lambda b, r, n: (b, r, n, 0)),
            pl.BlockSpec((None, None, qb, LANES), lambda b, r, n: (b, r, n, 0)),
        ],
        out_shape=[
            jax.ShapeDtypeStruct((B, dil, L, ATT_W), BF16),
            jax.ShapeDtypeStruct((B, dil, L, LANES), F32),
        ],
        compiler_params=pltpu.CompilerParams(
            dimension_semantics=("arbitrary", "arbitrary", "arbitrary"), vmem_limit_bytes=VMEM_LIMIT),
        name=f"attn_prompt_g{g}",
    )(qkv, qkv, qkv, qkv, qkv, jnp.asarray(b0), jnp.asarray(br))


def _merge_tail(x, gapa, sgb, zb, attn, wb_ref, wo_ref, g_ref, bt_ref):
    ab = (attn * zb.astype(F32)).astype(BF16)
    pb = _dot(ab, wb_ref[...])
    m = gapa.astype(F32) + sgb.astype(F32) * pb
    o = _dot(m.astype(BF16), wo_ref[...])
    return _layer_norm(ALPHA * x + o, g_ref[...], bt_ref[...])


def _merge_groups_kernel(x_ref, gapa_ref, sgb_ref, zb_ref, o1_ref, o2_ref, o3_ref, l1_ref, l2_ref, l3_ref,
                         ex_ref, wb_ref, wo_ref, g_ref, bt_ref, y_ref, nat_ref, natl_ref, *, tm):
    nsl = ATT_W // LANES
    for gi, (o_ref, l_ref) in enumerate(((o2_ref, l2_ref), (o3_ref, l3_ref))):
        dil = GROUPS[gi + 1][1]
        n = tm // dil
        for r in range(dil):
            blk = o_ref[r].astype(F32)
            for sl in range(nsl):
                nat_ref[gi, sl, pl.ds(r, n, stride=dil), :] = blk[:, sl * LANES:(sl + 1) * LANES]
            natl_ref[gi, pl.ds(r, n, stride=dil), :] = l_ref[r]
    l1, l2, l3 = l1_ref[...], natl_ref[0], natl_ref[1]
    mx = jnp.maximum(jnp.maximum(l1, l2), l3)
    e1, e2, e3 = jnp.exp(l1 - mx), jnp.exp(l2 - mx), jnp.exp(l3 - mx)
    inv = 1.0 / (e1 + e2 + e3)
    ex = ex_ref[...]
    w1, w2, w3 = (_dot((e * inv).astype(BF16), ex) for e in (e1, e2, e3))
    slabs = []
    for sl in range(nsl):
        cs = slice(sl * LANES, (sl + 1) * LANES)
        slabs.append(w1[:, cs] * o1_ref[:, cs].astype(F32) + w2[:, cs] * nat_ref[0, sl] + w3[:, cs] * nat_ref[1, sl])
    attn = jnp.concatenate(slabs, axis=1)
    y_ref[...] = _merge_tail(x_ref[...], gapa_ref[...], sgb_ref[...], zb_ref[...], attn,
                             wb_ref, wo_ref, g_ref, bt_ref)


def _merge_plain_kernel(x_ref, gapa_ref, sgb_ref, zb_ref, attn_ref, wb_ref, wo_ref, g_ref, bt_ref, y_ref):
    y_ref[...] = _merge_tail(x_ref[...], gapa_ref[...], sgb_ref[...], zb_ref[...], attn_ref[...],
                             wb_ref, wo_ref, g_ref, bt_ref)


def _head_expand_matrix():
    ex = np.zeros((LANES, ATT_W), np.float32)
    for h in range(N_SLOTS):
        ex[h, h * HEAD_DIM:(h + 1) * HEAD_DIM] = 1.0
    return ex


def _merge_weight_specs():
    return [_const_spec((ATT_W, D_MODEL), (0, 0)), _const_spec((D_MODEL, D_MODEL), (0, 0)),
            _const_spec((1, D_MODEL), (0, 0)), _const_spec((1, D_MODEL), (0, 0))]


def _merge_groups(x, gapa, sgb, zb, parts, wb_bf, wo_bf, ln_g, ln_b, tm):
    B, S, _ = x.shape
    tok = lambda width: pl.BlockSpec((None, tm, width), lambda b, s: (b, s, 0))
    grp = lambda dil, width: pl.BlockSpec((None, dil, tm // dil, width), lambda b, s: (b, 0, s, 0))
    (o1, l1), (o2, l2), (o3, l3) = parts
    d2, d3 = GROUPS[1][1], GROUPS[2][1]
    return pl.pallas_call(
        functools.partial(_merge_groups_kernel, tm=tm),
        grid=(B, S // tm),
        in_specs=[tok(D_MODEL), tok(D_MODEL), tok(D_MODEL), tok(ATT_W),
                  tok(ATT_W), grp(d2, ATT_W), grp(d3, ATT_W), tok(LANES), grp(d2, LANES), grp(d3, LANES),
                  _const_spec((LANES, ATT_W), (0, 0))] + _merge_weight_specs(),
        out_specs=tok(D_MODEL),
        out_shape=jax.ShapeDtypeStruct((B, S, D_MODEL), F32),
        scratch_shapes=[pltpu.VMEM((2, ATT_W // LANES, tm, LANES), F32), pltpu.VMEM((2, tm, LANES), F32)],
        compiler_params=pltpu.CompilerParams(
            dimension_semantics=("arbitrary", "arbitrary"), vmem_limit_bytes=VMEM_LIMIT),
        name="merge_groups",
    )(x, gapa, sgb, zb, o1.reshape(B, S, ATT_W), o2, o3, l1.reshape(B, S, LANES), l2, l3,
      jnp.asarray(_head_expand_matrix(), BF16), wb_bf, wo_bf, ln_g, ln_b)


def _merge_plain(x2, gapa, sgb, zb, attn, wb_bf, wo_bf, ln_g, ln_b):
    T = x2.shape[0]
    tok = lambda width: pl.BlockSpec((T, width), lambda t: (0, 0))
    return pl.pallas_call(
        _merge_plain_kernel,
        grid=(1,),
        in_specs=[tok(D_MODEL), tok(D_MODEL), tok(D_MODEL), tok(ATT_W), tok(ATT_W)] + _merge_weight_specs(),
        out_specs=tok(D_MODEL),
        out_shape=jax.ShapeDtypeStruct((T, D_MODEL), F32),
        compiler_params=pltpu.CompilerParams(dimension_semantics=("arbitrary",), vmem_limit_bytes=VMEM_LIMIT),
        name="merge_plain",
    )(x2, gapa, sgb, zb, attn, wb_bf, wo_bf, ln_g, ln_b)


def _sample_proj_kernel(x_ref, w_ref, b_ref, h_ref):
    h_ref[...] = _dot(x_ref[...].astype(BF16), w_ref[...]) + b_ref[...]


def _sample_proj(xs2, wbf, bsc):
    T = xs2.shape[0]
    cw = 1024
    return pl.pallas_call(
        _sample_proj_kernel,
        grid=(IN_W // cw,),
        in_specs=[
            pl.BlockSpec((T, D_MODEL), lambda j: (0, 0)),
            pl.BlockSpec((D_MODEL, cw), lambda j: (0, j)),
            pl.BlockSpec((1, cw), lambda j: (0, j)),
        ],
        out_specs=pl.BlockSpec((T, cw), lambda j: (0, j)),
        out_shape=jax.ShapeDtypeStruct((T, IN_W), F32),
        compiler_params=pltpu.CompilerParams(dimension_semantics=("arbitrary",), vmem_limit_bytes=VMEM_LIMIT),
        name="sample_proj",
    )(xs2, wbf, bsc)


def _sample_conv_kernel(h3_ref, hga_ref, hzb_ref, hgb_ref, st_ref, cw_ref, cb_ref, lg_ref, lb_ref, wa_ref,
                        gapa_ref, zb_ref, sgb_ref, cst_ref, hist_ref, slab_ref, cvt_ref, *, nb, tq):
    nh = CONV_WIDTH - 1
    a = h3_ref[:, O_AVAL:O_AVAL + CONV_CH]
    gt = h3_ref[:, O_AGATE:O_AGATE + CONV_CH]
    u = a * _sigmoid(gt)
    for sl in range(N_SLAB):
        slab_ref[sl] = u[:, sl * LANES:(sl + 1) * LANES]
    hist_ref[0:nh] = st_ref[...]
    for t in range(tq):
        hist_ref[nh + t] = jnp.concatenate(
            [slab_ref[sl, pl.ds(t, nb, stride=tq), :] for sl in range(N_SLAB)], axis=1)

    def conv_step(t, carry):
        acc = jnp.zeros((nb, CONV_CH), F32)
        for j in range(CONV_WIDTH):
            wj = jnp.concatenate([cw_ref[j * 8:(j + 1) * 8, :]] * (nb // 8), axis=0)
            acc = acc + hist_ref[t + j] * wj
        cvt_ref[t] = acc + cb_ref[...]
        return carry

    lax.fori_loop(0, tq, conv_step, 0)
    for t in range(tq):
        cvt = cvt_ref[t]
        for sl in range(N_SLAB):
            slab_ref[sl, pl.ds(t, nb, stride=tq), :] = cvt[:, sl * LANES:(sl + 1) * LANES]
    cst_ref[...] = hist_ref[tq:tq + nh]

    cv = jnp.concatenate([slab_ref[sl] for sl in range(N_SLAB)], axis=1)
    y = _layer_norm(cv, lg_ref[...], lb_ref[...])
    ca = (_silu(y) * _silu(h3_ref[:, O_ZA:O_ZA + CONV_CH])).astype(BF16)
    gapa_ref[...] = (_sigmoid(hga_ref[...]) * _dot(ca, wa_ref[...])).astype(BF16)
    zb_ref[...] = _silu(hzb_ref[...]).astype(BF16)
    sgb_ref[...] = _sigmoid(hgb_ref[...]).astype(BF16)


def _sample_conv_branch(hs, state_t, cw8, conv_b, lg, lb, wa_bf, nb, tq):
    T = nb * tq
    nh = CONV_WIDTH - 1
    col = lambda width, off: pl.BlockSpec((T, width), lambda i: (0, off // width))
    full = lambda shape: pl.BlockSpec(shape, lambda i: (0,) * len(shape))
    return pl.pallas_call(
        functools.partial(_sample_conv_kernel, nb=nb, tq=tq),
        grid=(1,),
        in_specs=[
            col(3 * CONV_CH, 0), col(D_MODEL, O_GA), col(ATT_W, O_ZB), col(D_MODEL, O_GB),
            full((nh, nb, CONV_CH)), full((CONV_WIDTH * 8, CONV_CH)),
            full((1, CONV_CH)), full((1, CONV_CH)), full((1, CONV_CH)), full((CONV_CH, D_MODEL)),
        ],
        out_specs=[full((T, D_MODEL)), full((T, ATT_W)), full((T, D_MODEL)), full((nh, nb, CONV_CH))],
        out_shape=[
            jax.ShapeDtypeStruct((T, D_MODEL), BF16),
            jax.ShapeDtypeStruct((T, ATT_W), BF16),
            jax.ShapeDtypeStruct((T, D_MODEL), BF16),
            jax.ShapeDtypeStruct((nh, nb, CONV_CH), F32),
        ],
        scratch_shapes=[pltpu.VMEM((nh + tq, nb, CONV_CH), F32), pltpu.VMEM((N_SLAB, T, LANES), F32),
                        pltpu.VMEM((tq, nb, CONV_CH), F32)],
        compiler_params=pltpu.CompilerParams(dimension_semantics=("arbitrary",), vmem_limit_bytes=VMEM_LIMIT),
        name="sample_conv_branch",
    )(hs, hs, hs, hs, state_t, cw8, conv_b, lg, lb, wa_bf)


def _sample_bias_tables(g, tq):
    win, dil = GROUPS[g]
    slopes = _alibi_slopes()
    tabc = np.full((N_SLOTS * tq, win), NEG, np.float32)
    tabn = np.full((N_SLOTS * tq, tq), NEG, np.float32)
    pos = np.arange(win)
    for h in range(N_SLOTS):
        for t in range(tq):
            row = h * tq + t
            dist = win + t - pos
            ok = (dist % dil == 0) & (dist // dil <= win // dil)
            tabc[row, ok] = -slopes[h] * dist[ok].astype(np.float32)
            for t2 in range(t + 1):
                if (t - t2) % dil == 0:
                    tabn[row, t2] = -slopes[h] * np.float32(t - t2)
    return tabc, tabn


def _sample_attn_kernel(q_ref, k_ref, v_ref, c1_ref, c2_ref, c3_ref, tc1_ref, tc2_ref, tc3_ref,
                        tn1_ref, tn2_ref, tn3_ref, hm_ref, o_ref, u1_ref, u2_ref, u3_ref,
                        p1_ref, p2_ref, p3_ref, pn_ref, den_ref, *, tq):
    kv = pl.program_id(1)
    caches = (c1_ref, c2_ref, c3_ref)
    upd = (u1_ref, u2_ref, u3_ref)
    probs = (p1_ref, p2_ref, p3_ref)
    tabc = (tc1_ref, tc2_ref, tc3_ref)
    tabn = (tn1_ref, tn2_ref, tn3_ref)
    hm = hm_ref[...]
    lane = lax.broadcasted_iota(jnp.int32, (ATT_W, LANES), 1)

    def write_shifted(g, new_rows):
        win = GROUPS[g][0]
        rolled = pltpu.roll(caches[g][...], win - tq, axis=1)
        padded = jnp.concatenate([jnp.zeros((LANES - tq, ATT_W), F32), new_rows], axis=0)
        last = jnp.where(lane >= LANES - tq, padded.T, rolled[:, win - LANES:])
        if win > LANES:
            upd[g][:, 0:win - LANES] = rolled[:, 0:win - LANES]
        upd[g][:, win - LANES:] = last

    @pl.when(kv == 0)
    def _():
        scores, mx = [], None
        for g in range(N_GROUPS):
            c0 = g * ATT_W
            qbd = (jnp.concatenate([q_ref[:, c0:c0 + ATT_W]] * N_SLOTS, axis=0) * hm).astype(BF16)
            sc = _dot(qbd, caches[g][...].astype(BF16)) + tabc[g][...]
            sn = _dot_nt(qbd, k_ref[:, c0:c0 + ATT_W].astype(BF16)) + tabn[g][...]
            m = jnp.maximum(jnp.max(sc, axis=1, keepdims=True), jnp.max(sn, axis=1, keepdims=True))
            mx = m if mx is None else jnp.maximum(mx, m)
            scores.append((sc, sn))
        den = jnp.zeros((N_SLOTS * tq, 1), F32)
        for g in range(N_GROUPS):
            sc, sn = scores[g]
            p = jnp.exp(sc - mx)
            pn = jnp.exp(sn - mx)
            den = den + jnp.sum(p, axis=1, keepdims=True) + jnp.sum(pn, axis=1, keepdims=True)
            probs[g][...] = p.astype(BF16)
            pn_ref[g] = pn
        den_ref[...] = den
        for g in range(N_GROUPS):
            write_shifted(g, k_ref[:, g * ATT_W:(g + 1) * ATT_W])

    @pl.when(kv == 1)
    def _():
        acc = jnp.zeros((N_SLOTS * tq, ATT_W), F32)
        for g in range(N_GROUPS):
            c0 = g * ATT_W
            acc = acc + _dot_nt(probs[g][...], caches[g][...].astype(BF16))
            acc = acc + _dot(pn_ref[g].astype(BF16), v_ref[:, c0:c0 + ATT_W].astype(BF16))
        out = (acc / den_ref[...]) * hm
        o_ref[...] = jnp.sum(out.reshape(N_SLOTS, tq, ATT_W), axis=0)
        for g in range(N_GROUPS):
            write_shifted(g, v_ref[:, g * ATT_W:(g + 1) * ATT_W])


def _sample_attn(hs3, caches_t, nb, tq):
    cspecs, tcs, tns = [], [], []
    for g, (win, _) in enumerate(GROUPS):
        cspecs.append(pl.BlockSpec((None, None, ATT_W, win), lambda b, kv: (b, kv, 0, 0)))
        tc, tn = _sample_bias_tables(g, tq)
        tcs.append(jnp.asarray(tc))
        tns.append(jnp.asarray(tn))
    hm = np.zeros((N_SLOTS * tq, ATT_W), np.float32)
    for h in range(N_SLOTS):
        hm[h * tq:(h + 1) * tq, h * HEAD_DIM:(h + 1) * HEAD_DIM] = 1.0
    qkv = lambda off: pl.BlockSpec((None, tq, QKV_W), lambda b, kv: (b, 0, off // QKV_W))
    const = lambda a: _const_spec(a.shape, (0,) * a.ndim)
    nrow = N_SLOTS * tq
    return pl.pallas_call(
        functools.partial(_sample_attn_kernel, tq=tq),
        grid=(nb, 2),
        in_specs=[qkv(O_Q), qkv(O_K), qkv(O_V)] + cspecs + [const(t) for t in tcs] + [const(t) for t in tns]
        + [_const_spec(hm.shape, (0, 0))],
        out_specs=[pl.BlockSpec((None, tq, ATT_W), lambda b, kv: (b, 0, 0))] + cspecs,
        out_shape=[jax.ShapeDtypeStruct((nb, tq, ATT_W), F32)]
        + [jax.ShapeDtypeStruct((nb, 2, ATT_W, win), F32) for win, _ in GROUPS],
        scratch_shapes=[pltpu.VMEM((nrow, win), BF16) for win, _ in GROUPS]
        + [pltpu.VMEM((N_GROUPS, nrow, tq), F32), pltpu.VMEM((nrow, 1), F32)],
        compiler_params=pltpu.CompilerParams(
            dimension_semantics=("arbitrary", "arbitrary"), vmem_limit_bytes=VMEM_LIMIT),
        name="sample_attn",
    )(hs3, hs3, hs3, *caches_t, *tcs, *tns, jnp.asarray(hm))


def _to_feature_major(c):
    n, w = c.shape[0], c.shape[1]
    return jnp.transpose(c, (0, 2, 3, 4, 1)).reshape(n, 2, ATT_W, w)


def _from_feature_major(c):
    n, _, _, w = c.shape
    return jnp.transpose(c.reshape(n, 2, N_SLOTS, HEAD_DIM, w), (0, 4, 1, 2, 3))


def kernel(x_prompt, x_sample, cache_kv_w128, cache_kv_w512, cache_kv_w2048, state_conv, w_in, b_in, conv_w, conv_b,
           conv_ln_g, conv_ln_b, w_a, w_b, w_out, ln_g, ln_b):
    B, S, _ = x_prompt.shape
    nb, tq, _ = x_sample.shape
    caches = (cache_kv_w128, cache_kv_w512, cache_kv_w2048)
    tm = 512
    for (win, dil), c in zip(GROUPS, caches):
        assert c.shape[1] == win and S % (dil * Q_BLK) == 0 and S >= win and tm % (dil * 16) == 0
    assert tq == 8 and nb % 8 == 0 and state_conv.shape[1] == CONV_WIDTH - 1 and S % tm == 0

    col_scale = np.ones((IN_W,), np.float32)
    col_scale[O_Q:O_K] = HEAD_DIM ** -0.5
    wbf = (w_in * col_scale).astype(BF16)
    b1 = b_in * col_scale
    bsc = b1.reshape(1, IN_W)
    wkt, wvt = wbf[:, O_K:O_V].T, wbf[:, O_V:O_ZB].T
    bkt, bvt = b1[O_K:O_V].reshape(QKV_W, 1), b1[O_V:O_ZB].reshape(QKV_W, 1)
    cw8 = jnp.broadcast_to(conv_w[:, None, :], (CONV_WIDTH, 8, CONV_CH)).reshape(CONV_WIDTH * 8, CONV_CH)
    cb, clg, clb = (t.reshape(1, CONV_CH) for t in (conv_b, conv_ln_g, conv_ln_b))
    lg, lb = ln_g.reshape(1, D_MODEL), ln_b.reshape(1, D_MODEL)
    wa_bf, wb_bf, wo_bf = w_a.astype(BF16), w_b.astype(BF16), w_out.astype(BF16)

    gapa, cst_p = _conv_branch_prompt(x_prompt, wbf, bsc, cw8, cb, clg, clb, wa_bf, tm)
    qkv1, qkv2, qkv3, zb, sgb = _qkv_prompt(x_prompt, wbf, bsc, tm)
    kvt = _kv_window(x_prompt, wkt, wvt, bkt, bvt, tm)
    parts = [_attn_prompt(t, g) for g, t in enumerate((qkv1, qkv2, qkv3))]
    y_p = _merge_groups(x_prompt, gapa, sgb, zb, parts, wb_bf, wo_bf, lg, lb, tm)
    kv_p = [_from_feature_major(t) for t in kvt]
    conv_p = cst_p[:, HOFF:, :]

    Ts = nb * tq
    hs = _sample_proj(x_sample.reshape(Ts, D_MODEL), wbf, bsc)
    state_t = jnp.transpose(state_conv, (1, 0, 2))
    gapa_s, zb_s, sgb_s, cst_t = _sample_conv_branch(hs, state_t, cw8, cb, clg, clb, wa_bf, nb, tq)
    hs3 = hs.reshape(nb, tq, IN_W)
    attn_s, *kvs_t = _sample_attn(hs3, [_to_feature_major(c) for c in caches], nb, tq)
    y_s = _merge_plain(x_sample.reshape(Ts, D_MODEL), gapa_s, sgb_s, zb_s, attn_s.reshape(Ts, ATT_W),
                       wb_bf, wo_bf, lg, lb).reshape(nb, tq, D_MODEL)
    kv_s = [_from_feature_major(t) for t in kvs_t]
    conv_s = jnp.transpose(cst_t, (1, 0, 2))
    return (y_p, y_s, kv_p[0], kv_p[1], kv_p[2], conv_p, kv_s[0], kv_s[1], kv_s[2], conv_s)
```

```python
import functools

import numpy as np
import jax
import jax.numpy as jnp
from jax import lax
from jax.experimental import pallas as pl
from jax.experimental.pallas import tpu as pltpu

F32 = jnp.float32
BF16 = jnp.bfloat16

D_MODEL = 1024
CONV_CH = 1024
CONV_WIDTH = 31
N_SLOTS = 8
HEAD_DIM = 64
GROUPS = ((128, 1), (512, 4), (2048, 16))
N_GROUPS = len(GROUPS)
ATT_W = N_SLOTS * HEAD_DIM
QKV_W = N_GROUPS * ATT_W
Q_BLK = 128
DEPTH = 1
ALPHA = (2.0 * DEPTH) ** 0.25
LN_EPS = 1e-5
NEG = -1e30

O_AVAL, O_AGATE, O_ZA = 0, CONV_CH, 2 * CONV_CH
O_Q = 3 * CONV_CH
O_K = O_Q + QKV_W
O_V = O_K + QKV_W
O_ZB = O_V + QKV_W
O_GA = O_ZB + ATT_W
O_GB = O_GA + D_MODEL
IN_W = O_GB + D_MODEL

HIST = 32
HOFF = HIST - (CONV_WIDTH - 1)
LANES = 128
VMEM_LIMIT = 56 * 1024 * 1024


def _dot(a, b):
    return jnp.dot(a, b, preferred_element_type=F32)


def _dot_nt(a, b):
    return lax.dot_general(a, b, (((1,), (1,)), ((), ())), preferred_element_type=F32)


def _sigmoid(x):
    return 0.5 * jnp.tanh(0.5 * x) + 0.5


def _silu(x):
    h = 0.5 * x
    return h * jnp.tanh(h) + h


def _layer_norm(x, g, b):
    mu = jnp.mean(x, axis=-1, keepdims=True)
    xc = x - mu
    var = jnp.mean(xc * xc, axis=-1, keepdims=True)
    return xc * lax.rsqrt(var + LN_EPS) * g + b


def _const_spec(shape, index):
    return pl.BlockSpec(shape, lambda *_: index, pipeline_mode=pl.Buffered(1))


def _alibi_slopes():
    return (2.0 ** (-8.0 * np.arange(1, N_SLOTS + 1, dtype=np.float64) / N_SLOTS)).astype(np.float32)


SHIFT_ROWS = 64


def _shift_cache_block(c_ref, new_ref, u_ref, g, tq):
    win = GROUPS[g][0]
    lane = lax.broadcasted_iota(jnp.int32, (SHIFT_ROWS, LANES), 1)
    for i in range(c_ref.shape[0] // ATT_W):
        padded = jnp.concatenate([jnp.zeros((LANES - tq, ATT_W), F32), new_ref[i, :, g * ATT_W:(g + 1) * ATT_W]], axis=0)
        new_t = padded.T
        for r in range(0, ATT_W, SHIFT_ROWS):
            rows = slice(i * ATT_W + r, i * ATT_W + r + SHIFT_ROWS)
            rolled = pltpu.roll(c_ref[rows, :], win - tq, axis=1)
            if win > LANES:
                u_ref[rows, 0:win - LANES] = rolled[:, 0:win - LANES]
            u_ref[rows, win - LANES:] = jnp.where(lane >= LANES - tq, new_t[r:r + SHIFT_ROWS, :], rolled[:, win - LANES:])


def _conv_branch_kernel(x_ref, w3_ref, wga_ref, b3_ref, bga_ref, cw_ref, cb_ref, lg_ref, lb_ref, wa_ref,
                        new_ref, c1_ref, c2_ref, c3_ref,
                        gapa_ref, cst_ref, u1_ref, u2_ref, u3_ref, uh_ref, cv_ref, ca_ref, *, tm, ns, tq):
    s = pl.program_id(1)
    ch = 256
    for g, (c_ref, u_ref) in enumerate(((c1_ref, u1_ref), (c2_ref, u2_ref), (c3_ref, u3_ref))):
        _shift_cache_block(c_ref, new_ref, u_ref, g, tq)

    @pl.when(s == 0)
    def _():
        uh_ref[0:HIST, :] = jnp.zeros((HIST, CONV_CH), F32)

    xb = x_ref[...].astype(BF16)

    def proj(c0, n):
        return _dot(xb, w3_ref[:, c0:c0 + n]) + b3_ref[:, c0:c0 + n]

    for c in range(0, CONV_CH, ch):
        a = proj(O_AVAL + c, ch)
        gt = proj(O_AGATE + c, ch)
        uh_ref[HIST:HIST + tm, c:c + ch] = a * _sigmoid(gt)

    rc = 64

    def conv_rows(i, carry):
        base = pl.multiple_of(HIST - 8 + i * rc, 8)
        for c in range(0, CONV_CH, ch):
            run = None
            for b in range(7, -1, -1):
                part = None
                for a in range((CONV_WIDTH - 1 - b) // 8 + 1):
                    j = CONV_WIDTH - 1 - (8 * a + b)
                    wj = jnp.concatenate([cw_ref[j * 8:(j + 1) * 8, c:c + ch]] * (rc // 8 + 1), axis=0)
                    term = uh_ref[pl.ds(base - 8 * a, rc + 8), c:c + ch] * wj
                    part = term if part is None else part + term
                run = part if run is None else part + pltpu.roll(run, 1, axis=0)
            cv_ref[pl.ds(pl.multiple_of(i * rc, rc), rc), c:c + ch] = run[8:, :] + cb_ref[:, c:c + ch]
        return carry

    lax.fori_loop(0, tm // rc, conv_rows, 0)

    @pl.when(s == ns - 1)
    def _():
        cst_ref[...] = uh_ref[tm:tm + HIST, :]

    uh_ref[0:HIST, :] = uh_ref[tm:tm + HIST, :]

    x = cv_ref[...]
    mu = jnp.mean(x, axis=-1, keepdims=True)
    xc = x - mu
    var = jnp.mean(xc * xc, axis=-1, keepdims=True)
    rstd = lax.rsqrt(var + LN_EPS)
    for c in range(0, CONV_CH, ch):
        y = (cv_ref[:, c:c + ch] - mu) * rstd * lg_ref[:, c:c + ch] + lb_ref[:, c:c + ch]
        za = proj(O_ZA + c, ch)
        ca_ref[:, c:c + ch] = (_silu(y) * _silu(za)).astype(BF16)

    ca = ca_ref[...]
    for c in range(0, D_MODEL, ch):
        pa = _dot(ca, wa_ref[:, c:c + ch])
        ga = _dot(xb, wga_ref[:, c:c + ch]) + bga_ref[:, c:c + ch]
        gapa_ref[:, c:c + ch] = (_sigmoid(ga) * pa).astype(BF16)


def _conv_branch_prompt(x, wbf, bsc, cw8, conv_b, lg, lb, wa_bf, kv_new, caches_t, tm):
    B, S, _ = x.shape
    ns = S // tm
    nb, _, tq, _ = kv_new.shape
    pairs = 2 * nb
    assert pairs % (B * ns) == 0
    ppb = pairs // (B * ns)
    kern = functools.partial(_conv_branch_kernel, tm=tm, ns=ns, tq=tq)
    cache_specs = [pl.BlockSpec((ppb * ATT_W, win), lambda b, s: (b * ns + s, 0)) for win, _ in GROUPS]
    cache_rows = [c.reshape(pairs * ATT_W, c.shape[-1]) for c in caches_t]
    return pl.pallas_call(
        kern,
        grid=(B, ns),
        in_specs=[
            pl.BlockSpec((None, tm, D_MODEL), lambda b, s: (b, s, 0)),
            _const_spec((D_MODEL, 3 * CONV_CH), (0, 0)),
            _const_spec((D_MODEL, D_MODEL), (0, O_GA // D_MODEL)),
            _const_spec((1, 3 * CONV_CH), (0, 0)),
            _const_spec((1, D_MODEL), (0, O_GA // D_MODEL)),
            _const_spec((CONV_WIDTH * 8, CONV_CH), (0, 0)),
            _const_spec((1, CONV_CH), (0, 0)),
            _const_spec((1, CONV_CH), (0, 0)),
            _const_spec((1, CONV_CH), (0, 0)),
            _const_spec((CONV_CH, D_MODEL), (0, 0)),
            pl.BlockSpec((ppb, tq, QKV_W), lambda b, s: (b * ns + s, 0, 0)),
        ] + cache_specs,
        out_specs=[
            pl.BlockSpec((None, tm, D_MODEL), lambda b, s: (b, s, 0)),
            pl.BlockSpec((None, HIST, CONV_CH), lambda b, s: (b, 0, 0)),
        ] + cache_specs,
        out_shape=[
            jax.ShapeDtypeStruct((B, S, D_MODEL), BF16),
            jax.ShapeDtypeStruct((B, HIST, CONV_CH), F32),
        ] + [jax.ShapeDtypeStruct(c.shape, F32) for c in cache_rows],
        scratch_shapes=[
            pltpu.VMEM((HIST + tm, CONV_CH), F32),
            pltpu.VMEM((tm, CONV_CH), F32),
            pltpu.VMEM((tm, CONV_CH), BF16),
        ],
        compiler_params=pltpu.CompilerParams(
            dimension_semantics=("arbitrary", "arbitrary"), vmem_limit_bytes=VMEM_LIMIT),
        name="conv_branch_prompt",
    )(x, wbf, wbf, bsc, bsc, cw8, conv_b, lg, lb, wa_bf, kv_new.reshape(pairs, tq, QKV_W), *cache_rows)


N_SLAB = D_MODEL // LANES


def _qkv_kernel(x_ref, wq_ref, wk_ref, wv_ref, wzb_ref, wgb_ref, bq_ref, bk_ref, bv_ref, bzb_ref, bgb_ref,
                o1_ref, o2_ref, o3_ref, zb_ref, sgb_ref, xs_ref, *, tm):
    xb = x_ref[...].astype(BF16)
    for sl in range(N_SLAB):
        xs_ref[sl] = x_ref[:, sl * LANES:(sl + 1) * LANES]
    outs = (o1_ref, o2_ref, o3_ref)
    for g, (_, dil) in enumerate(GROUPS):
        n = tm // dil
        if dil == 1:
            lhs = xb
        else:
            lhs = jnp.concatenate(
                [jnp.concatenate([xs_ref[sl, pl.ds(r, n, stride=dil), :] for sl in range(N_SLAB)], axis=1)
                 for r in range(dil)], axis=0).astype(BF16)
        c0 = g * ATT_W
        for part, (w_ref, b_ref) in enumerate(((wq_ref, bq_ref), (wk_ref, bk_ref), (wv_ref, bv_ref))):
            res = _dot(lhs, w_ref[:, c0:c0 + ATT_W]) + b_ref[:, c0:c0 + ATT_W]
            outs[g][:, :, part * ATT_W:(part + 1) * ATT_W] = res.astype(BF16).reshape(dil, n, ATT_W)
    zb = _dot(xb, wzb_ref[...]) + bzb_ref[...]
    zb_ref[...] = _silu(zb).astype(BF16)
    for c in range(0, D_MODEL, 512):
        gb = _dot(xb, wgb_ref[:, c:c + 512]) + bgb_ref[:, c:c + 512]
        sgb_ref[:, c:c + 512] = _sigmoid(gb).astype(BF16)


def _qkv_prompt(x, wbf, bsc, tm):
    B, S, _ = x.shape
    ns = S // tm
    tok = lambda width: pl.BlockSpec((None, tm, width), lambda b, s: (b, s, 0))
    grp = lambda dil: pl.BlockSpec((None, dil, tm // dil, QKV_W), lambda b, s: (b, 0, s, 0))
    wspec = lambda width, off: _const_spec((D_MODEL, width), (0, off // width))
    bspec = lambda width, off: _const_spec((1, width), (0, off // width))
    return pl.pallas_call(
        functools.partial(_qkv_kernel, tm=tm),
        grid=(B, ns),
        in_specs=[
            tok(D_MODEL),
            wspec(QKV_W, O_Q), wspec(QKV_W, O_K), wspec(QKV_W, O_V), wspec(ATT_W, O_ZB), wspec(D_MODEL, O_GB),
            bspec(QKV_W, O_Q), bspec(QKV_W, O_K), bspec(QKV_W, O_V), bspec(ATT_W, O_ZB), bspec(D_MODEL, O_GB),
        ],
        out_specs=[grp(d) for _, d in GROUPS] + [tok(ATT_W), tok(D_MODEL)],
        out_shape=[jax.ShapeDtypeStruct((B, d, S // d, QKV_W), BF16) for _, d in GROUPS] + [
            jax.ShapeDtypeStruct((B, S, ATT_W), BF16),
            jax.ShapeDtypeStruct((B, S, D_MODEL), BF16),
        ],
        scratch_shapes=[pltpu.VMEM((N_SLAB, tm, LANES), F32)],
        compiler_params=pltpu.CompilerParams(
            dimension_semantics=("arbitrary", "arbitrary"), vmem_limit_bytes=VMEM_LIMIT),
        name="qkv_prompt",
    )(x, wbf, wbf, wbf, wbf, wbf, bsc, bsc, bsc, bsc, bsc)


def _kv_window_kernel(x_ref, wkt_ref, wvt_ref, bkt_ref, bvt_ref, o1_ref, o2_ref, o3_ref, *, tw, nw, keeps):
    j = pl.program_id(1)
    xb = x_ref[...].astype(BF16)
    for g, o_ref in enumerate((o1_ref, o2_ref, o3_ref)):
        rows = min(keeps[g], tw)
        nwt = keeps[g] // rows
        c0 = g * ATT_W

        @pl.when(j >= nw - nwt)
        def _(o_ref=o_ref, rows=rows, c0=c0):
            kt = _dot_nt(wkt_ref[c0:c0 + ATT_W, :], xb) + bkt_ref[c0:c0 + ATT_W, :]
            vt = _dot_nt(wvt_ref[c0:c0 + ATT_W, :], xb) + bvt_ref[c0:c0 + ATT_W, :]
            o_ref[0] = kt[:, tw - rows:]
            o_ref[1] = vt[:, tw - rows:]


def _kv_window(x, wkt, wvt, bkt, bvt, tw):
    B, S, _ = x.shape
    keeps = tuple(min(w, S) for w, _ in GROUPS)
    nw = max(keeps) // tw
    first = S // tw - nw

    def out_spec(keep):
        rows = min(keep, tw)
        f = nw - keep // rows
        return pl.BlockSpec((None, 2, ATT_W, rows), lambda b, j: (b, 0, 0, jnp.maximum(j - f, 0)))

    return pl.pallas_call(
        functools.partial(_kv_window_kernel, tw=tw, nw=nw, keeps=keeps),
        grid=(B, nw),
        in_specs=[
            pl.BlockSpec((None, tw, D_MODEL), lambda b, j: (b, first + j, 0)),
            _const_spec((QKV_W, D_MODEL), (0, 0)), _const_spec((QKV_W, D_MODEL), (0, 0)),
            _const_spec((QKV_W, 1), (0, 0)), _const_spec((QKV_W, 1), (0, 0)),
        ],
        out_specs=[out_spec(k) for k in keeps],
        out_shape=[jax.ShapeDtypeStruct((B, 2, ATT_W, k), F32) for k in keeps],
        compiler_params=pltpu.CompilerParams(
            dimension_semantics=("arbitrary", "arbitrary"), vmem_limit_bytes=VMEM_LIMIT),
        name="kv_window",
    )(x, wkt, wvt, bkt, bvt)


def _prompt_bias_tables(dil):
    slopes = _alibi_slopes()
    i = np.arange(Q_BLK)[:, None]
    j = np.arange(2 * Q_BLK)[None, :]
    delta = i + Q_BLK - j
    band = (delta >= 0) & (delta <= Q_BLK)
    bias = -slopes[:, None, None] * (delta * dil).astype(np.float32)[None]
    rest = np.where(band[None], bias, np.float32(NEG)).astype(np.float32)
    first = np.where((band & (j >= Q_BLK))[None], bias, np.float32(NEG)).astype(np.float32)
    return np.stack([first, rest]), rest


def _attn_kernel(q_ref, kp_ref, kc_ref, vp_ref, vc_ref, b0_ref, br_ref, o_ref, lse_ref, *, nsub, rpb):
    lane = lax.broadcasted_iota(jnp.int32, (Q_BLK, LANES), 1)
    lo = lane < HEAD_DIM
    m_lo = lo.astype(F32).astype(BF16)
    m_hi = (1.0 - lo.astype(F32)).astype(BF16)
    for rr, i in ((rr, i) for rr in range(rpb) for i in range(nsub)):
        r0 = i * Q_BLK
        lse_tile = jnp.zeros((Q_BLK, LANES), F32)
        for hp in range(N_SLOTS // 2):
            l0 = hp * LANES
            q2 = q_ref[rr, r0:r0 + Q_BLK, l0:l0 + LANES]
            if i == 0:
                k2 = jnp.concatenate([kp_ref[rr, :, l0:l0 + LANES], kc_ref[rr, 0:Q_BLK, l0:l0 + LANES]], axis=0)
                v2 = jnp.concatenate([vp_ref[rr, :, l0:l0 + LANES], vc_ref[rr, 0:Q_BLK, l0:l0 + LANES]], axis=0)
                bref = b0_ref
            else:
                k2 = kc_ref[rr, r0 - Q_BLK:r0 + Q_BLK, l0:l0 + LANES]
                v2 = vc_ref[rr, r0 - Q_BLK:r0 + Q_BLK, l0:l0 + LANES]
                bref = br_ref
            outs = []
            for par in range(2):
                h = 2 * hp + par
                qm = q2 * (m_lo if par == 0 else m_hi)
                sc = _dot_nt(qm, k2) + bref[h]
                mx = jnp.max(sc, axis=1, keepdims=True)
                p = jnp.exp(sc - mx)
                den = jnp.sum(p, axis=1, keepdims=True)
                o = _dot(p.astype(BF16), v2)
                outs.append(o / den)
                lse_tile = jnp.where(lane == h, mx + jnp.log(den), lse_tile)
            o_ref[rr, r0:r0 + Q_BLK, l0:l0 + LANES] = jnp.where(lo, outs[0], outs[1]).astype(BF16)
        lse_ref[rr, r0:r0 + Q_BLK, :] = lse_tile


def _attn_prompt(qkv, g):
    B, dil, L, _ = qkv.shape
    qb = min(512, L)
    nq = L // qb
    nsub = qb // Q_BLK
    rpb = min(dil, max(1, 512 // qb))
    b0, br = _prompt_bias_tables(dil)
    cur = lambda part: pl.BlockSpec((None, rpb, qb, ATT_W), lambda b, r, n: (b, r, n, part))
    prev = lambda part: pl.BlockSpec((None, rpb, Q_BLK, ATT_W),
                                     lambda b, r, n: (b, r, jnp.maximum(n * nsub - 1, 0), part))
    return pl.pallas_call(
        functools.partial(_attn_kernel, nsub=nsub, rpb=rpb),
        grid=(B, dil // rpb, nq),
        in_specs=[
            cur(0), prev(1), cur(1), prev(2), cur(2),
            pl.BlockSpec((None, N_SLOTS, Q_BLK, 2 * Q_BLK), lambda b, r, n: (jnp.minimum(n, 1), 0, 0, 0)),
            _const_spec((N_SLOTS, Q_BLK, 2 * Q_BLK), (0, 0, 0)),
        ],
        out_specs=[
            pl.BlockSpec((None, rpb, qb, ATT_W), lambda b, r, n: (b, r, n, 0)),
            pl.BlockSpec((None, rpb, qb, LANES), lambda b, r, n: (b, r, n, 0)),
        ],
        out_shape=[
            jax.ShapeDtypeStruct((B, dil, L, ATT_W), BF16),
            jax.ShapeDtypeStruct((B, dil, L, LANES), F32),
        ],
        compiler_params=pltpu.CompilerParams(
            dimension_semantics=("arbitrary", "arbitrary", "arbitrary"), vmem_limit_bytes=VMEM_LIMIT),
        name=f"attn_prompt_g{g}",
    )(qkv, qkv, qkv, qkv, qkv, jnp.asarray(b0), jnp.asarray(br))


def _merge_tail(x, gapa, sgb, zb, attn, wb_ref, wo_ref, g_ref, bt_ref):
    ab = (attn * zb.astype(F32)).astype(BF16)
    pb = _dot(ab, wb_ref[...])
    m = gapa.astype(F32) + sgb.astype(F32) * pb
    o = _dot(m.astype(BF16), wo_ref[...])
    return _layer_norm(ALPHA * x + o, g_ref[...], bt_ref[...])


def _merge_groups_kernel(x_ref, gapa_ref, sgb_ref, zb_ref, o1_ref, o2_ref, o3_ref, l1_ref, l2_ref, l3_ref,
                         ex_ref, wb_ref, wo_ref, g_ref, bt_ref, y_ref, nat_ref, natl_ref, *, tm):
    nsl = ATT_W // LANES
    for gi, (o_ref, l_ref) in enumerate(((o2_ref, l2_ref), (o3_ref, l3_ref))):
        dil = GROUPS[gi + 1][1]
        n = tm // dil
        for r in range(dil):
            blk = o_ref[r].astype(F32)
            for sl in range(nsl):
                nat_ref[gi, sl, pl.ds(r, n, stride=dil), :] = blk[:, sl * LANES:(sl + 1) * LANES]
            natl_ref[gi, pl.ds(r, n, stride=dil), :] = l_ref[r]
    l1, l2, l3 = l1_ref[...], natl_ref[0], natl_ref[1]
    mx = jnp.maximum(jnp.maximum(l1, l2), l3)
    e1, e2, e3 = jnp.exp(l1 - mx), jnp.exp(l2 - mx), jnp.exp(l3 - mx)
    inv = 1.0 / (e1 + e2 + e3)
    ex = ex_ref[...]
    w1, w2, w3 = (_dot((e * inv).astype(BF16), ex) for e in (e1, e2, e3))
    slabs = []
    for sl in range(nsl):
        cs = slice(sl * LANES, (sl + 1) * LANES)
        slabs.append(w1[:, cs] * o1_ref[:, cs].astype(F32) + w2[:, cs] * nat_ref[0, sl] + w3[:, cs] * nat_ref[1, sl])
    attn = jnp.concatenate(slabs, axis=1)
    y_ref[...] = _merge_tail(x_ref[...], gapa_ref[...], sgb_ref[...], zb_ref[...], attn,
                             wb_ref, wo_ref, g_ref, bt_ref)


def _merge_plain_kernel(x_ref, gapa_ref, sgb_ref, zb_ref, attn_ref, wb_ref, wo_ref, g_ref, bt_ref, y_ref):
    y_ref[...] = _merge_tail(x_ref[...], gapa_ref[...], sgb_ref[...], zb_ref[...], attn_ref[...],
                             wb_ref, wo_ref, g_ref, bt_ref)


def _head_expand_matrix():
    ex = np.zeros((LANES, ATT_W), np.float32)
    for h in range(N_SLOTS):
        ex[h, h * HEAD_DIM:(h + 1) * HEAD_DIM] = 1.0
    return ex


def _merge_weight_specs():
    return [_const_spec((ATT_W, D_MODEL), (0, 0)), _const_spec((D_MODEL, D_MODEL), (0, 0)),
            _const_spec((1, D_MODEL), (0, 0)), _const_spec((1, D_MODEL), (0, 0))]


def _merge_groups(x, gapa, sgb, zb, parts, wb_bf, wo_bf, ln_g, ln_b, tm):
    B, S, _ = x.shape
    tok = lambda width: pl.BlockSpec((None, tm, width), lambda b, s: (b, s, 0))
    grp = lambda dil, width: pl.BlockSpec((None, dil, tm // dil, width), lambda b, s: (b, 0, s, 0))
    (o1, l1), (o2, l2), (o3, l3) = parts
    d2, d3 = GROUPS[1][1], GROUPS[2][1]
    return pl.pallas_call(
        functools.partial(_merge_groups_kernel, tm=tm),
        grid=(B, S // tm),
        in_specs=[tok(D_MODEL), tok(D_MODEL), tok(D_MODEL), tok(ATT_W),
                  tok(ATT_W), grp(d2, ATT_W), grp(d3, ATT_W), tok(LANES), grp(d2, LANES), grp(d3, LANES),
                  _const_spec((LANES, ATT_W), (0, 0))] + _merge_weight_specs(),
        out_specs=tok(D_MODEL),
        out_shape=jax.ShapeDtypeStruct((B, S, D_MODEL), F32),
        scratch_shapes=[pltpu.VMEM((2, ATT_W // LANES, tm, LANES), F32), pltpu.VMEM((2, tm, LANES), F32)],
        compiler_params=pltpu.CompilerParams(
            dimension_semantics=("arbitrary", "arbitrary"), vmem_limit_bytes=VMEM_LIMIT),
        name="merge_groups",
    )(x, gapa, sgb, zb, o1.reshape(B, S, ATT_W), o2, o3, l1.reshape(B, S, LANES), l2, l3,
      jnp.asarray(_head_expand_matrix(), BF16), wb_bf, wo_bf, ln_g, ln_b)


def _merge_plain(x2, gapa, sgb, zb, attn, wb_bf, wo_bf, ln_g, ln_b):
    T = x2.shape[0]
    tok = lambda width: pl.BlockSpec((T, width), lambda t: (0, 0))
    return pl.pallas_call(
        _merge_plain_kernel,
        grid=(1,),
        in_specs=[tok(D_MODEL), tok(D_MODEL), tok(D_MODEL), tok(ATT_W), tok(ATT_W)] + _merge_weight_specs(),
        out_specs=tok(D_MODEL),
        out_shape=jax.ShapeDtypeStruct((T, D_MODEL), F32),
        compiler_params=pltpu.CompilerParams(dimension_semantics=("arbitrary",), vmem_limit_bytes=VMEM_LIMIT),
        name="merge_plain",
    )(x2, gapa, sgb, zb, attn, wb_bf, wo_bf, ln_g, ln_b)


def _sample_proj_kernel(x_ref, w_ref, b_ref, h_ref):
    h_ref[...] = _dot(x_ref[...].astype(BF16), w_ref[...]) + b_ref[...]


def _sample_proj(xs2, wbf, bsc):
    T = xs2.shape[0]
    cw = 1024
    return pl.pallas_call(
        _sample_proj_kernel,
        grid=(IN_W // cw,),
        in_specs=[
            pl.BlockSpec((T, D_MODEL), lambda j: (0, 0)),
            pl.BlockSpec((D_MODEL, cw), lambda j: (0, j)),
            pl.BlockSpec((1, cw), lambda j: (0, j)),
        ],
        out_specs=pl.BlockSpec((T, cw), lambda j: (0, j)),
        out_shape=jax.ShapeDtypeStruct((T, IN_W), F32),
        compiler_params=pltpu.CompilerParams(dimension_semantics=("arbitrary",), vmem_limit_bytes=VMEM_LIMIT),
        name="sample_proj",
    )(xs2, wbf, bsc)


def _sample_conv_kernel(h3_ref, hga_ref, hzb_ref, hgb_ref, st_ref, cw_ref, cb_ref, lg_ref, lb_ref, wa_ref,
                        gapa_ref, zb_ref, sgb_ref, cst_ref, hist_ref, slab_ref, cvt_ref, *, nb, tq):
    nh = CONV_WIDTH - 1
    a = h3_ref[:, O_AVAL:O_AVAL + CONV_CH]
    gt = h3_ref[:, O_AGATE:O_AGATE + CONV_CH]
    u = a * _sigmoid(gt)
    for sl in range(N_SLAB):
        slab_ref[sl] = u[:, sl * LANES:(sl + 1) * LANES]
    hist_ref[0:nh] = st_ref[...]
    for t in range(tq):
        hist_ref[nh + t] = jnp.concatenate(
            [slab_ref[sl, pl.ds(t, nb, stride=tq), :] for sl in range(N_SLAB)], axis=1)

    def conv_step(t, carry):
        acc = jnp.zeros((nb, CONV_CH), F32)
        for j in range(CONV_WIDTH):
            wj = jnp.concatenate([cw_ref[j * 8:(j + 1) * 8, :]] * (nb // 8), axis=0)
            acc = acc + hist_ref[t + j] * wj
        cvt_ref[t] = acc + cb_ref[...]
        return carry

    lax.fori_loop(0, tq, conv_step, 0)
    for t in range(tq):
        cvt = cvt_ref[t]
        for sl in range(N_SLAB):
            slab_ref[sl, pl.ds(t, nb, stride=tq), :] = cvt[:, sl * LANES:(sl + 1) * LANES]
    cst_ref[...] = hist_ref[tq:tq + nh]

    cv = jnp.concatenate([slab_ref[sl] for sl in range(N_SLAB)], axis=1)
    y = _layer_norm(cv, lg_ref[...], lb_ref[...])
    ca = (_silu(y) * _silu(h3_ref[:, O_ZA:O_ZA + CONV_CH])).astype(BF16)
    gapa_ref[...] = (_sigmoid(hga_ref[...]) * _dot(ca, wa_ref[...])).astype(BF16)
    zb_ref[...] = _silu(hzb_ref[...]).astype(BF16)
    sgb_ref[...] = _sigmoid(hgb_ref[...]).astype(BF16)


def _sample_conv_branch(hs, state_t, cw8, conv_b, lg, lb, wa_bf, nb, tq):
    T = nb * tq
    nh = CONV_WIDTH - 1
    col = lambda width, off: pl.BlockSpec((T, width), lambda i: (0, off // width))
    full = lambda shape: pl.BlockSpec(shape, lambda i: (0,) * len(shape))
    return pl.pallas_call(
        functools.partial(_sample_conv_kernel, nb=nb, tq=tq),
        grid=(1,),
        in_specs=[
            col(3 * CONV_CH, 0), col(D_MODEL, O_GA), col(ATT_W, O_ZB), col(D_MODEL, O_GB),
            full((nh, nb, CONV_CH)), full((CONV_WIDTH * 8, CONV_CH)),
            full((1, CONV_CH)), full((1, CONV_CH)), full((1, CONV_CH)), full((CONV_CH, D_MODEL)),
        ],
        out_specs=[full((T, D_MODEL)), full((T, ATT_W)), full((T, D_MODEL)), full((nh, nb, CONV_CH))],
        out_shape=[
            jax.ShapeDtypeStruct((T, D_MODEL), BF16),
            jax.ShapeDtypeStruct((T, ATT_W), BF16),
            jax.ShapeDtypeStruct((T, D_MODEL), BF16),
            jax.ShapeDtypeStruct((nh, nb, CONV_CH), F32),
        ],
        scratch_shapes=[pltpu.VMEM((nh + tq, nb, CONV_CH), F32), pltpu.VMEM((N_SLAB, T, LANES), F32),
                        pltpu.VMEM((tq, nb, CONV_CH), F32)],
        compiler_params=pltpu.CompilerParams(dimension_semantics=("arbitrary",), vmem_limit_bytes=VMEM_LIMIT),
        name="sample_conv_branch",
    )(hs, hs, hs, hs, state_t, cw8, conv_b, lg, lb, wa_bf)


def _sample_bias_tables(g, tq):
    win, dil = GROUPS[g]
    slopes = _alibi_slopes()
    tabc = np.full((N_SLOTS * tq, win), NEG, np.float32)
    tabn = np.full((N_SLOTS * tq, tq), NEG, np.float32)
    pos = np.arange(win)
    for h in range(N_SLOTS):
        for t in range(tq):
            row = h * tq + t
            dist = win + t - pos
            ok = (dist % dil == 0) & (dist // dil <= win // dil)
            tabc[row, ok] = -slopes[h] * dist[ok].astype(np.float32)
            for t2 in range(t + 1):
                if (t - t2) % dil == 0:
                    tabn[row, t2] = -slopes[h] * np.float32(t - t2)
    return tabc, tabn


def _sample_attn_kernel(q_ref, k_ref, v_ref, c1_ref, c2_ref, c3_ref, tc1_ref, tc2_ref, tc3_ref,
                        tn1_ref, tn2_ref, tn3_ref, hm_ref, o_ref,
                        p1_ref, p2_ref, p3_ref, pn_ref, den_ref, *, tq):
    kv = pl.program_id(1)
    caches = (c1_ref, c2_ref, c3_ref)
    probs = (p1_ref, p2_ref, p3_ref)
    tabc = (tc1_ref, tc2_ref, tc3_ref)
    tabn = (tn1_ref, tn2_ref, tn3_ref)
    hm = hm_ref[...]

    @pl.when(kv == 0)
    def _():
        scores, mx = [], None
        for g in range(N_GROUPS):
            c0 = g * ATT_W
            qbd = (jnp.concatenate([q_ref[:, c0:c0 + ATT_W]] * N_SLOTS, axis=0) * hm).astype(BF16)
            sc = _dot(qbd, caches[g][...].astype(BF16)) + tabc[g][...]
            sn = _dot_nt(qbd, k_ref[:, c0:c0 + ATT_W].astype(BF16)) + tabn[g][...]
            m = jnp.maximum(jnp.max(sc, axis=1, keepdims=True), jnp.max(sn, axis=1, keepdims=True))
            mx = m if mx is None else jnp.maximum(mx, m)
            scores.append((sc, sn))
        den = jnp.zeros((N_SLOTS * tq, 1), F32)
        for g in range(N_GROUPS):
            sc, sn = scores[g]
            p = jnp.exp(sc - mx)
            pn = jnp.exp(sn - mx)
            den = den + jnp.sum(p, axis=1, keepdims=True) + jnp.sum(pn, axis=1, keepdims=True)
            probs[g][...] = p.astype(BF16)
            pn_ref[g] = pn
        den_ref[...] = den

    @pl.when(kv == 1)
    def _():
        acc = jnp.zeros((N_SLOTS * tq, ATT_W), F32)
        for g in range(N_GROUPS):
            c0 = g * ATT_W
            acc = acc + _dot_nt(probs[g][...], caches[g][...].astype(BF16))
            acc = acc + _dot(pn_ref[g].astype(BF16), v_ref[:, c0:c0 + ATT_W].astype(BF16))
        out = (acc / den_ref[...]) * hm
        o_ref[...] = jnp.sum(out.reshape(N_SLOTS, tq, ATT_W), axis=0)


def _sample_attn(hs3, caches_t, nb, tq):
    cspecs, tcs, tns = [], [], []
    for g, (win, _) in enumerate(GROUPS):
        cspecs.append(pl.BlockSpec((None, None, ATT_W, win), lambda b, kv: (b, kv, 0, 0)))
        tc, tn = _sample_bias_tables(g, tq)
        tcs.append(jnp.asarray(tc))
        tns.append(jnp.asarray(tn))
    hm = np.zeros((N_SLOTS * tq, ATT_W), np.float32)
    for h in range(N_SLOTS):
        hm[h * tq:(h + 1) * tq, h * HEAD_DIM:(h + 1) * HEAD_DIM] = 1.0
    qkv = lambda off: pl.BlockSpec((None, tq, QKV_W), lambda b, kv: (b, 0, off // QKV_W))
    const = lambda a: _const_spec(a.shape, (0,) * a.ndim)
    nrow = N_SLOTS * tq
    return pl.pallas_call(
        functools.partial(_sample_attn_kernel, tq=tq),
        grid=(nb, 2),
        in_specs=[qkv(O_Q), qkv(O_K), qkv(O_V)] + cspecs + [const(t) for t in tcs] + [const(t) for t in tns]
        + [_const_spec(hm.shape, (0, 0))],
        out_specs=pl.BlockSpec((None, tq, ATT_W), lambda b, kv: (b, 0, 0)),
        out_shape=jax.ShapeDtypeStruct((nb, tq, ATT_W), F32),
        scratch_shapes=[pltpu.VMEM((nrow, win), BF16) for win, _ in GROUPS]
        + [pltpu.VMEM((N_GROUPS, nrow, tq), F32), pltpu.VMEM((nrow, 1), F32)],
        compiler_params=pltpu.CompilerParams(
            dimension_semantics=("arbitrary", "arbitrary"), vmem_limit_bytes=VMEM_LIMIT),
        name="sample_attn",
    )(hs3, hs3, hs3, *caches_t, *tcs, *tns, jnp.asarray(hm))


def _to_feature_major(c):
    n, w = c.shape[0], c.shape[1]
    return jnp.transpose(c, (0, 2, 3, 4, 1)).reshape(n, 2, ATT_W, w)


def _from_feature_major(c):
    n, _, _, w = c.shape
    return jnp.transpose(c.reshape(n, 2, N_SLOTS, HEAD_DIM, w), (0, 4, 1, 2, 3))


def kernel(x_prompt, x_sample, cache_kv_w128, cache_kv_w512, cache_kv_w2048, state_conv, w_in, b_in, conv_w, conv_b,
           conv_ln_g, conv_ln_b, w_a, w_b, w_out, ln_g, ln_b):
    B, S, _ = x_prompt.shape
    nb, tq, _ = x_sample.shape
    caches = (cache_kv_w128, cache_kv_w512, cache_kv_w2048)
    tm = 512
    for (win, dil), c in zip(GROUPS, caches):
        assert c.shape[1] == win and S % (dil * Q_BLK) == 0 and S >= win and tm % (dil * 16) == 0
    assert tq == 8 and nb % 8 == 0 and state_conv.shape[1] == CONV_WIDTH - 1 and S % tm == 0

    col_scale = np.ones((IN_W,), np.float32)
    col_scale[O_Q:O_K] = HEAD_DIM ** -0.5
    wbf = (w_in * col_scale).astype(BF16)
    b1 = b_in * col_scale
    bsc = b1.reshape(1, IN_W)
    wkt, wvt = w_in[:, O_K:O_V].T.astype(BF16), w_in[:, O_V:O_ZB].T.astype(BF16)
    bkt, bvt = b1[O_K:O_V].reshape(QKV_W, 1), b1[O_V:O_ZB].reshape(QKV_W, 1)
    cw8 = jnp.broadcast_to(conv_w[:, None, :], (CONV_WIDTH, 8, CONV_CH)).reshape(CONV_WIDTH * 8, CONV_CH)
    cb, clg, clb = (t.reshape(1, CONV_CH) for t in (conv_b, conv_ln_g, conv_ln_b))
    lg, lb = ln_g.reshape(1, D_MODEL), ln_b.reshape(1, D_MODEL)
    wa_bf, wb_bf, wo_bf = w_a.astype(BF16), w_b.astype(BF16), w_out.astype(BF16)

    Ts = nb * tq
    hs = _sample_proj(x_sample.reshape(Ts, D_MODEL), wbf, bsc)
    hs3 = hs.reshape(nb, tq, IN_W)
    kv_new = jnp.stack([hs3[:, :, O_K:O_V], hs3[:, :, O_V:O_ZB]], axis=1)
    caches_t = [_to_feature_major(c) for c in caches]

    gapa, cst_p, *kvs_rows = _conv_branch_prompt(x_prompt, wbf, bsc, cw8, cb, clg, clb, wa_bf, kv_new, caches_t, tm)
    qkv1, qkv2, qkv3, zb, sgb = _qkv_prompt(x_prompt, wbf, bsc, tm)
    kvt = _kv_window(x_prompt, wkt, wvt, bkt, bvt, tm)
    parts = [_attn_prompt(t, g) for g, t in enumerate((qkv1, qkv2, qkv3))]
    y_p = _merge_groups(x_prompt, gapa, sgb, zb, parts, wb_bf, wo_bf, lg, lb, tm)
    kv_p = [_from_feature_major(t) for t in kvt]
    conv_p = cst_p[:, HOFF:, :]

    state_t = jnp.transpose(state_conv, (1, 0, 2))
    gapa_s, zb_s, sgb_s, cst_t = _sample_conv_branch(hs, state_t, cw8, cb, clg, clb, wa_bf, nb, tq)
    attn_s = _sample_attn(hs3, caches_t, nb, tq)
    y_s = _merge_plain(x_sample.reshape(Ts, D_MODEL), gapa_s, sgb_s, zb_s, attn_s.reshape(Ts, ATT_W),
                       wb_bf, wo_bf, lg, lb).reshape(nb, tq, D_MODEL)
    kv_s = [_from_feature_major(t.reshape(nb, 2, ATT_W, t.shape[-1])) for t in kvs_rows]
    conv_s = jnp.transpose(cst_t, (1, 0, 2))
    return (y_p, y_s, kv_p[0], kv_p[1], kv_p[2], conv_p, kv_s[0], kv_s[1], kv_s[2], conv_s)
```

```python
import functools

import numpy as np
import jax
import jax.numpy as jnp
from jax import lax
from jax.experimental import pallas as pl
from jax.experimental.pallas import tpu as pltpu

F32 = jnp.float32
BF16 = jnp.bfloat16

D_MODEL = 1024
CONV_CH = 1024
CONV_WIDTH = 31
N_SLOTS = 8
HEAD_DIM = 64
GROUPS = ((128, 1), (512, 4), (2048, 16))
N_GROUPS = len(GROUPS)
ATT_W = N_SLOTS * HEAD_DIM
QKV_W = N_GROUPS * ATT_W
Q_BLK = 128
DEPTH = 1
ALPHA = (2.0 * DEPTH) ** 0.25
LN_EPS = 1e-5
NEG = -1e30

O_AVAL, O_AGATE, O_ZA = 0, CONV_CH, 2 * CONV_CH
O_Q = 3 * CONV_CH
O_K = O_Q + QKV_W
O_V = O_K + QKV_W
O_ZB = O_V + QKV_W
O_GA = O_ZB + ATT_W
O_GB = O_GA + D_MODEL
IN_W = O_GB + D_MODEL

HIST = 32
HOFF = HIST - (CONV_WIDTH - 1)
LANES = 128
VMEM_LIMIT = 56 * 1024 * 1024


def _dot(a, b):
    return jnp.dot(a, b, preferred_element_type=F32)


def _dot_nt(a, b):
    return lax.dot_general(a, b, (((1,), (1,)), ((), ())), preferred_element_type=F32)


def _sigmoid(x):
    return 0.5 * jnp.tanh(0.5 * x) + 0.5


def _silu(x):
    h = 0.5 * x
    return h * jnp.tanh(h) + h


def _layer_norm(x, g, b):
    mu = jnp.mean(x, axis=-1, keepdims=True)
    xc = x - mu
    var = jnp.mean(xc * xc, axis=-1, keepdims=True)
    return xc * lax.rsqrt(var + LN_EPS) * g + b


def _const_spec(shape, index):
    return pl.BlockSpec(shape, lambda *_: index, pipeline_mode=pl.Buffered(1))


def _alibi_slopes():
    return (2.0 ** (-8.0 * np.arange(1, N_SLOTS + 1, dtype=np.float64) / N_SLOTS)).astype(np.float32)


SHIFT_ROWS = 64


def _stage_new_rows(new_ref, newt_ref, tq):
    for g in range(N_GROUPS):
        for i in range(new_ref.shape[0]):
            rows = new_ref[i, :, g * ATT_W:(g + 1) * ATT_W]
            padded = jnp.concatenate([jnp.zeros((LANES - tq, ATT_W), F32), rows], axis=0)
            newt_ref[g, i * ATT_W:(i + 1) * ATT_W, :] = padded.T


def _shift_cache_rows(c_ref, newt_ref, u_ref, g, r0, tq):
    win = GROUPS[g][0]
    lane = lax.broadcasted_iota(jnp.int32, (SHIFT_ROWS, LANES), 1)
    rows = pl.ds(r0, SHIFT_ROWS)
    rolled = pltpu.roll(c_ref[rows, :], win - tq, axis=1)
    if win > LANES:
        u_ref[rows, 0:win - LANES] = rolled[:, 0:win - LANES]
    u_ref[rows, win - LANES:] = jnp.where(lane >= LANES - tq, newt_ref[g, rows, :], rolled[:, win - LANES:])


def _conv_branch_kernel(x_ref, w3_ref, wga_ref, b3_ref, bga_ref, cw_ref, cb_ref, lg_ref, lb_ref, wa_ref,
                        new_ref, c1_ref, c2_ref, c3_ref,
                        gapa_ref, cst_ref, u1_ref, u2_ref, u3_ref, uh_ref, cv_ref, ca_ref, newt_ref,
                        *, tm, ns, tq):
    s = pl.program_id(1)
    ch = 256
    _stage_new_rows(new_ref, newt_ref, tq)

    @pl.when(s == 0)
    def _():
        uh_ref[0:HIST, :] = jnp.zeros((HIST, CONV_CH), F32)

    xb = x_ref[...].astype(BF16)

    def proj(c0, n):
        return _dot(xb, w3_ref[:, c0:c0 + n]) + b3_ref[:, c0:c0 + n]

    for c in range(0, CONV_CH, ch):
        a = proj(O_AVAL + c, ch)
        gt = proj(O_AGATE + c, ch)
        uh_ref[HIST:HIST + tm, c:c + ch] = a * _sigmoid(gt)

    rc = 64

    def conv_rows(i, carry):
        base = pl.multiple_of(HIST - 8 + i * rc, 8)
        for c in range(0, CONV_CH, ch):
            run = None
            for b in range(7, -1, -1):
                part = None
                for a in range((CONV_WIDTH - 1 - b) // 8 + 1):
                    j = CONV_WIDTH - 1 - (8 * a + b)
                    wj = jnp.concatenate([cw_ref[j * 8:(j + 1) * 8, c:c + ch]] * (rc // 8 + 1), axis=0)
                    term = uh_ref[pl.ds(base - 8 * a, rc + 8), c:c + ch] * wj
                    part = term if part is None else part + term
                run = part if run is None else part + pltpu.roll(run, 1, axis=0)
            cv_ref[pl.ds(pl.multiple_of(i * rc, rc), rc), c:c + ch] = run[8:, :] + cb_ref[:, c:c + ch]
        for g, (c_ref, u_ref) in enumerate(((c1_ref, u1_ref), (c2_ref, u2_ref), (c3_ref, u3_ref))):
            for pair in range(c_ref.shape[0] // ATT_W):
                r0 = pl.multiple_of(pair * ATT_W + i * SHIFT_ROWS, SHIFT_ROWS)
                _shift_cache_rows(c_ref, newt_ref, u_ref, g, r0, tq)
        return carry

    assert (tm // rc) * SHIFT_ROWS == ATT_W
    lax.fori_loop(0, tm // rc, conv_rows, 0)

    @pl.when(s == ns - 1)
    def _():
        cst_ref[...] = uh_ref[tm:tm + HIST, :]

    uh_ref[0:HIST, :] = uh_ref[tm:tm + HIST, :]

    x = cv_ref[...]
    mu = jnp.mean(x, axis=-1, keepdims=True)
    xc = x - mu
    var = jnp.mean(xc * xc, axis=-1, keepdims=True)
    rstd = lax.rsqrt(var + LN_EPS)
    for c in range(0, CONV_CH, ch):
        y = (cv_ref[:, c:c + ch] - mu) * rstd * lg_ref[:, c:c + ch] + lb_ref[:, c:c + ch]
        za = proj(O_ZA + c, ch)
        ca_ref[:, c:c + ch] = (_silu(y) * _silu(za)).astype(BF16)

    ca = ca_ref[...]
    for c in range(0, D_MODEL, ch):
        pa = _dot(ca, wa_ref[:, c:c + ch])
        ga = _dot(xb, wga_ref[:, c:c + ch]) + bga_ref[:, c:c + ch]
        gapa_ref[:, c:c + ch] = (_sigmoid(ga) * pa).astype(BF16)


def _conv_branch_prompt(x, wbf, bsc, cw8, conv_b, lg, lb, wa_bf, kv_new, caches_t, tm):
    B, S, _ = x.shape
    ns = S // tm
    nb, _, tq, _ = kv_new.shape
    pairs = 2 * nb
    assert pairs % (B * ns) == 0
    ppb = pairs // (B * ns)
    kern = functools.partial(_conv_branch_kernel, tm=tm, ns=ns, tq=tq)
    cache_specs = [pl.BlockSpec((ppb * ATT_W, win), lambda b, s: (b * ns + s, 0)) for win, _ in GROUPS]
    cache_rows = [c.reshape(pairs * ATT_W, c.shape[-1]) for c in caches_t]
    return pl.pallas_call(
        kern,
        grid=(B, ns),
        in_specs=[
            pl.BlockSpec((None, tm, D_MODEL), lambda b, s: (b, s, 0)),
            _const_spec((D_MODEL, 3 * CONV_CH), (0, 0)),
            _const_spec((D_MODEL, D_MODEL), (0, O_GA // D_MODEL)),
            _const_spec((1, 3 * CONV_CH), (0, 0)),
            _const_spec((1, D_MODEL), (0, O_GA // D_MODEL)),
            _const_spec((CONV_WIDTH * 8, CONV_CH), (0, 0)),
            _const_spec((1, CONV_CH), (0, 0)),
            _const_spec((1, CONV_CH), (0, 0)),
            _const_spec((1, CONV_CH), (0, 0)),
            _const_spec((CONV_CH, D_MODEL), (0, 0)),
            pl.BlockSpec((ppb, tq, QKV_W), lambda b, s: (b * ns + s, 0, 0)),
        ] + cache_specs,
        out_specs=[
            pl.BlockSpec((None, tm, D_MODEL), lambda b, s: (b, s, 0)),
            pl.BlockSpec((None, HIST, CONV_CH), lambda b, s: (b, 0, 0)),
        ] + cache_specs,
        out_shape=[
            jax.ShapeDtypeStruct((B, S, D_MODEL), BF16),
            jax.ShapeDtypeStruct((B, HIST, CONV_CH), F32),
        ] + [jax.ShapeDtypeStruct(c.shape, F32) for c in cache_rows],
        scratch_shapes=[
            pltpu.VMEM((HIST + tm, CONV_CH), F32),
            pltpu.VMEM((tm, CONV_CH), F32),
            pltpu.VMEM((tm, CONV_CH), BF16),
            pltpu.VMEM((N_GROUPS, ppb * ATT_W, LANES), F32),
        ],
        compiler_params=pltpu.CompilerParams(
            dimension_semantics=("arbitrary", "arbitrary"), vmem_limit_bytes=VMEM_LIMIT),
        name="conv_branch_prompt",
    )(x, wbf, wbf, bsc, bsc, cw8, conv_b, lg, lb, wa_bf, kv_new.reshape(pairs, tq, QKV_W), *cache_rows)


N_SLAB = D_MODEL // LANES


def _qkv_kernel(x_ref, wq_ref, wk_ref, wv_ref, wzb_ref, wgb_ref, bq_ref, bk_ref, bv_ref, bzb_ref, bgb_ref,
                o1_ref, o2_ref, o3_ref, zb_ref, sgb_ref, xs_ref, *, tm):
    xb = x_ref[...].astype(BF16)
    for sl in range(N_SLAB):
        xs_ref[sl] = x_ref[:, sl * LANES:(sl + 1) * LANES]
    outs = (o1_ref, o2_ref, o3_ref)
    for g, (_, dil) in enumerate(GROUPS):
        n = tm // dil
        if dil == 1:
            lhs = xb
        else:
            lhs = jnp.concatenate(
                [jnp.concatenate([xs_ref[sl, pl.ds(r, n, stride=dil), :] for sl in range(N_SLAB)], axis=1)
                 for r in range(dil)], axis=0).astype(BF16)
        c0 = g * ATT_W
        for part, (w_ref, b_ref) in enumerate(((wq_ref, bq_ref), (wk_ref, bk_ref), (wv_ref, bv_ref))):
            res = _dot(lhs, w_ref[:, c0:c0 + ATT_W]) + b_ref[:, c0:c0 + ATT_W]
            outs[g][:, :, part * ATT_W:(part + 1) * ATT_W] = res.astype(BF16).reshape(dil, n, ATT_W)
    zb = _dot(xb, wzb_ref[...]) + bzb_ref[...]
    zb_ref[...] = _silu(zb).astype(BF16)
    for c in range(0, D_MODEL, 512):
        gb = _dot(xb, wgb_ref[:, c:c + 512]) + bgb_ref[:, c:c + 512]
        sgb_ref[:, c:c + 512] = _sigmoid(gb).astype(BF16)


def _qkv_prompt(x, wbf, bsc, tm):
    B, S, _ = x.shape
    ns = S // tm
    tok = lambda width: pl.BlockSpec((None, tm, width), lambda b, s: (b, s, 0))
    grp = lambda dil: pl.BlockSpec((None, dil, tm // dil, QKV_W), lambda b, s: (b, 0, s, 0))
    wspec = lambda width, off: _const_spec((D_MODEL, width), (0, off // width))
    bspec = lambda width, off: _const_spec((1, width), (0, off // width))
    return pl.pallas_call(
        functools.partial(_qkv_kernel, tm=tm),
        grid=(B, ns),
        in_specs=[
            tok(D_MODEL),
            wspec(QKV_W, O_Q), wspec(QKV_W, O_K), wspec(QKV_W, O_V), wspec(ATT_W, O_ZB), wspec(D_MODEL, O_GB),
            bspec(QKV_W, O_Q), bspec(QKV_W, O_K), bspec(QKV_W, O_V), bspec(ATT_W, O_ZB), bspec(D_MODEL, O_GB),
        ],
        out_specs=[grp(d) for _, d in GROUPS] + [tok(ATT_W), tok(D_MODEL)],
        out_shape=[jax.ShapeDtypeStruct((B, d, S // d, QKV_W), BF16) for _, d in GROUPS] + [
            jax.ShapeDtypeStruct((B, S, ATT_W), BF16),
            jax.ShapeDtypeStruct((B, S, D_MODEL), BF16),
        ],
        scratch_shapes=[pltpu.VMEM((N_SLAB, tm, LANES), F32)],
        compiler_params=pltpu.CompilerParams(
            dimension_semantics=("arbitrary", "arbitrary"), vmem_limit_bytes=VMEM_LIMIT),
        name="qkv_prompt",
    )(x, wbf, wbf, wbf, wbf, wbf, bsc, bsc, bsc, bsc, bsc)


def _kv_window_kernel(x_ref, wkt_ref, wvt_ref, bkt_ref, bvt_ref, o1_ref, o2_ref, o3_ref, *, tw, nw, keeps):
    j = pl.program_id(1)
    xb = x_ref[...].astype(BF16)
    for g, o_ref in enumerate((o1_ref, o2_ref, o3_ref)):
        rows = min(keeps[g], tw)
        nwt = keeps[g] // rows
        c0 = g * ATT_W

        @pl.when(j >= nw - nwt)
        def _(o_ref=o_ref, rows=rows, c0=c0):
            kt = _dot_nt(wkt_ref[c0:c0 + ATT_W, :], xb) + bkt_ref[c0:c0 + ATT_W, :]
            vt = _dot_nt(wvt_ref[c0:c0 + ATT_W, :], xb) + bvt_ref[c0:c0 + ATT_W, :]
            o_ref[0] = kt[:, tw - rows:]
            o_ref[1] = vt[:, tw - rows:]


def _kv_window(x, wkt, wvt, bkt, bvt, tw):
    B, S, _ = x.shape
    keeps = tuple(min(w, S) for w, _ in GROUPS)
    nw = max(keeps) // tw
    first = S // tw - nw

    def out_spec(keep):
        rows = min(keep, tw)
        f = nw - keep // rows
        return pl.BlockSpec((None, 2, ATT_W, rows), lambda b, j: (b, 0, 0, jnp.maximum(j - f, 0)))

    return pl.pallas_call(
        functools.partial(_kv_window_kernel, tw=tw, nw=nw, keeps=keeps),
        grid=(B, nw),
        in_specs=[
            pl.BlockSpec((None, tw, D_MODEL), lambda b, j: (b, first + j, 0)),
            _const_spec((QKV_W, D_MODEL), (0, 0)), _const_spec((QKV_W, D_MODEL), (0, 0)),
            _const_spec((QKV_W, 1), (0, 0)), _const_spec((QKV_W, 1), (0, 0)),
        ],
        out_specs=[out_spec(k) for k in keeps],
        out_shape=[jax.ShapeDtypeStruct((B, 2, ATT_W, k), F32) for k in keeps],
        compiler_params=pltpu.CompilerParams(
            dimension_semantics=("arbitrary", "arbitrary"), vmem_limit_bytes=VMEM_LIMIT),
        name="kv_window",
    )(x, wkt, wvt, bkt, bvt)


def _prompt_bias_tables(dil):
    slopes = _alibi_slopes()
    i = np.arange(Q_BLK)[:, None]
    j = np.arange(2 * Q_BLK)[None, :]
    delta = i + Q_BLK - j
    band = (delta >= 0) & (delta <= Q_BLK)
    bias = -slopes[:, None, None] * (delta * dil).astype(np.float32)[None]
    rest = np.where(band[None], bias, np.float32(NEG)).astype(np.float32)
    first = np.where((band & (j >= Q_BLK))[None], bias, np.float32(NEG)).astype(np.float32)
    return np.stack([first, rest]), rest


def _attn_kernel(q_ref, kp_ref, kc_ref, vp_ref, vc_ref, b0_ref, br_ref, o_ref, lse_ref, *, nsub, rpb):
    lane = lax.broadcasted_iota(jnp.int32, (Q_BLK, LANES), 1)
    lo = lane < HEAD_DIM
    m_lo = lo.astype(F32).astype(BF16)
    m_hi = (1.0 - lo.astype(F32)).astype(BF16)
    ones = jnp.ones((2 * Q_BLK, LANES), BF16)
    for rr, i in ((rr, i) for rr in range(rpb) for i in range(nsub)):
        r0 = i * Q_BLK
        lse_tile = jnp.zeros((Q_BLK, LANES), F32)
        for hp in range(N_SLOTS // 2):
            l0 = hp * LANES
            q2 = q_ref[rr, r0:r0 + Q_BLK, l0:l0 + LANES]
            if i == 0:
                k2 = jnp.concatenate([kp_ref[rr, :, l0:l0 + LANES], kc_ref[rr, 0:Q_BLK, l0:l0 + LANES]], axis=0)
                v2 = jnp.concatenate([vp_ref[rr, :, l0:l0 + LANES], vc_ref[rr, 0:Q_BLK, l0:l0 + LANES]], axis=0)
                bref = b0_ref
            else:
                k2 = kc_ref[rr, r0 - Q_BLK:r0 + Q_BLK, l0:l0 + LANES]
                v2 = vc_ref[rr, r0 - Q_BLK:r0 + Q_BLK, l0:l0 + LANES]
                bref = br_ref
            outs = []
            for par in range(2):
                h = 2 * hp + par
                qm = q2 * (m_lo if par == 0 else m_hi)
                sc = _dot_nt(qm, k2) + bref[h]
                mx = jnp.max(sc, axis=1, keepdims=True)
                p = jnp.exp(sc - mx).astype(BF16)
                oe = _dot(p, jnp.concatenate([v2, ones], axis=1))
                outs.append(oe[:, 0:LANES])
                lse_tile = jnp.where(lane == h, mx, lse_tile)
                lse_tile = jnp.where(lane == N_SLOTS + h, oe[:, LANES:], lse_tile)
            o_ref[rr, r0:r0 + Q_BLK, l0:l0 + LANES] = jnp.where(lo, outs[0], outs[1]).astype(BF16)
        lse_ref[rr, r0:r0 + Q_BLK, :] = lse_tile


def _attn_prompt(qkv, g):
    B, dil, L, _ = qkv.shape
    qb = min(512, L)
    nq = L // qb
    nsub = qb // Q_BLK
    rpb = min(dil, max(1, 512 // qb))
    b0, br = _prompt_bias_tables(dil)
    cur = lambda part: pl.BlockSpec((None, rpb, qb, ATT_W), lambda b, r, n: (b, r, n, part))
    prev = lambda part: pl.BlockSpec((None, rpb, Q_BLK, ATT_W),
                                     lambda b, r, n: (b, r, jnp.maximum(n * nsub - 1, 0), part))
    return pl.pallas_call(
        functools.partial(_attn_kernel, nsub=nsub, rpb=rpb),
        grid=(B, dil // rpb, nq),
        in_specs=[
            cur(0), prev(1), cur(1), prev(2), cur(2),
            pl.BlockSpec((None, N_SLOTS, Q_BLK, 2 * Q_BLK), lambda b, r, n: (jnp.minimum(n, 1), 0, 0, 0)),
            _const_spec((N_SLOTS, Q_BLK, 2 * Q_BLK), (0, 0, 0)),
        ],
        out_specs=[
            pl.BlockSpec((None, rpb, qb, ATT_W), lambda b, r, n: (b, r, n, 0)),
            pl.BlockSpec((None, rpb, qb, LANES), lambda b, r, n: (b, r, n, 0)),
        ],
        out_shape=[
            jax.ShapeDtypeStruct((B, dil, L, ATT_W), BF16),
            jax.ShapeDtypeStruct((B, dil, L, LANES), F32),
        ],
        compiler_params=pltpu.CompilerParams(
            dimension_semantics=("arbitrary", "arbitrary", "arbitrary"), vmem_limit_bytes=VMEM_LIMIT),
        name=f"attn_prompt_g{g}",
    )(qkv, qkv, qkv, qkv, qkv, jnp.asarray(b0), jnp.asarray(br))


def _merge_tail(x, gapa, sgb, zb, attn, wb_ref, wo_ref, g_ref, bt_ref):
    ab = (attn * zb.astype(F32)).astype(BF16)
    pb = _dot(ab, wb_ref[...])
    m = gapa.astype(F32) + sgb.astype(F32) * pb
    o = _dot(m.astype(BF16), wo_ref[...])
    return _layer_norm(ALPHA * x + o, g_ref[...], bt_ref[...])


def _merge_groups_kernel(x_ref, gapa_ref, sgb_ref, zb_ref, o1_ref, o2_ref, o3_ref, l1_ref, l2_ref, l3_ref,
                         ex_ref, wb_ref, wo_ref, g_ref, bt_ref, y_ref, nat_ref, natl_ref, *, tm):
    nsl = ATT_W // LANES
    for gi, (o_ref, l_ref) in enumerate(((o2_ref, l2_ref), (o3_ref, l3_ref))):
        dil = GROUPS[gi + 1][1]
        n = tm // dil
        for r in range(dil):
            blk = o_ref[r].astype(F32)
            for sl in range(nsl):
                nat_ref[gi, sl, pl.ds(r, n, stride=dil), :] = blk[:, sl * LANES:(sl + 1) * LANES]
            natl_ref[gi, pl.ds(r, n, stride=dil), :] = l_ref[r]
    sides = (l1_ref[...], natl_ref[0], natl_ref[1])
    mx = jnp.maximum(jnp.maximum(sides[0], sides[1]), sides[2])
    es = [jnp.exp(t - mx) for t in sides]
    dens = [pltpu.roll(t, LANES - N_SLOTS, axis=1) for t in sides]
    total = es[0] * dens[0] + es[1] * dens[1] + es[2] * dens[2]
    head_lane = lax.broadcasted_iota(jnp.int32, total.shape, 1) < N_SLOTS
    inv = jnp.where(head_lane, 1.0 / total, 0.0)
    ex = ex_ref[...]
    w1, w2, w3 = (_dot(jnp.where(head_lane, e * inv, 0.0).astype(BF16), ex) for e in es)
    slabs = []
    for sl in range(nsl):
        cs = slice(sl * LANES, (sl + 1) * LANES)
        slabs.append(w1[:, cs] * o1_ref[:, cs].astype(F32) + w2[:, cs] * nat_ref[0, sl] + w3[:, cs] * nat_ref[1, sl])
    attn = jnp.concatenate(slabs, axis=1)
    y_ref[...] = _merge_tail(x_ref[...], gapa_ref[...], sgb_ref[...], zb_ref[...], attn,
                             wb_ref, wo_ref, g_ref, bt_ref)


def _merge_plain_kernel(x_ref, gapa_ref, sgb_ref, zb_ref, attn_ref, wb_ref, wo_ref, g_ref, bt_ref, y_ref):
    y_ref[...] = _merge_tail(x_ref[...], gapa_ref[...], sgb_ref[...], zb_ref[...], attn_ref[...],
                             wb_ref, wo_ref, g_ref, bt_ref)


def _head_expand_matrix():
    ex = np.zeros((LANES, ATT_W), np.float32)
    for h in range(N_SLOTS):
        ex[h, h * HEAD_DIM:(h + 1) * HEAD_DIM] = 1.0
    return ex


def _merge_weight_specs():
    return [_const_spec((ATT_W, D_MODEL), (0, 0)), _const_spec((D_MODEL, D_MODEL), (0, 0)),
            _const_spec((1, D_MODEL), (0, 0)), _const_spec((1, D_MODEL), (0, 0))]


def _merge_groups(x, gapa, sgb, zb, parts, wb_bf, wo_bf, ln_g, ln_b, tm):
    B, S, _ = x.shape
    tok = lambda width: pl.BlockSpec((None, tm, width), lambda b, s: (b, s, 0))
    grp = lambda dil, width: pl.BlockSpec((None, dil, tm // dil, width), lambda b, s: (b, 0, s, 0))
    (o1, l1), (o2, l2), (o3, l3) = parts
    d2, d3 = GROUPS[1][1], GROUPS[2][1]
    return pl.pallas_call(
        functools.partial(_merge_groups_kernel, tm=tm),
        grid=(B, S // tm),
        in_specs=[tok(D_MODEL), tok(D_MODEL), tok(D_MODEL), tok(ATT_W),
                  tok(ATT_W), grp(d2, ATT_W), grp(d3, ATT_W), tok(LANES), grp(d2, LANES), grp(d3, LANES),
                  _const_spec((LANES, ATT_W), (0, 0))] + _merge_weight_specs(),
        out_specs=tok(D_MODEL),
        out_shape=jax.ShapeDtypeStruct((B, S, D_MODEL), F32),
        scratch_shapes=[pltpu.VMEM((2, ATT_W // LANES, tm, LANES), F32), pltpu.VMEM((2, tm, LANES), F32)],
        compiler_params=pltpu.CompilerParams(
            dimension_semantics=("arbitrary", "arbitrary"), vmem_limit_bytes=VMEM_LIMIT),
        name="merge_groups",
    )(x, gapa, sgb, zb, o1.reshape(B, S, ATT_W), o2, o3, l1.reshape(B, S, LANES), l2, l3,
      jnp.asarray(_head_expand_matrix(), BF16), wb_bf, wo_bf, ln_g, ln_b)


def _merge_plain(x2, gapa, sgb, zb, attn, wb_bf, wo_bf, ln_g, ln_b):
    T = x2.shape[0]
    tok = lambda width: pl.BlockSpec((T, width), lambda t: (0, 0))
    return pl.pallas_call(
        _merge_plain_kernel,
        grid=(1,),
        in_specs=[tok(D_MODEL), tok(D_MODEL), tok(D_MODEL), tok(ATT_W), tok(ATT_W)] + _merge_weight_specs(),
        out_specs=tok(D_MODEL),
        out_shape=jax.ShapeDtypeStruct((T, D_MODEL), F32),
        compiler_params=pltpu.CompilerParams(dimension_semantics=("arbitrary",), vmem_limit_bytes=VMEM_LIMIT),
        name="merge_plain",
    )(x2, gapa, sgb, zb, attn, wb_bf, wo_bf, ln_g, ln_b)


def _sample_proj_kernel(x_ref, w_ref, b_ref, h_ref):
    h_ref[...] = _dot(x_ref[...].astype(BF16), w_ref[...]) + b_ref[...]


def _sample_proj(xs2, wbf, bsc):
    T = xs2.shape[0]
    cw = 1024
    return pl.pallas_call(
        _sample_proj_kernel,
        grid=(IN_W // cw,),
        in_specs=[
            pl.BlockSpec((T, D_MODEL), lambda j: (0, 0)),
            pl.BlockSpec((D_MODEL, cw), lambda j: (0, j)),
            pl.BlockSpec((1, cw), lambda j: (0, j)),
        ],
        out_specs=pl.BlockSpec((T, cw), lambda j: (0, j)),
        out_shape=jax.ShapeDtypeStruct((T, IN_W), F32),
        compiler_params=pltpu.CompilerParams(dimension_semantics=("arbitrary",), vmem_limit_bytes=VMEM_LIMIT),
        name="sample_proj",
    )(xs2, wbf, bsc)


def _sample_conv_kernel(h3_ref, hga_ref, hzb_ref, hgb_ref, st_ref, cw_ref, cb_ref, lg_ref, lb_ref, wa_ref,
                        gapa_ref, zb_ref, sgb_ref, cst_ref, hist_ref, slab_ref, cvt_ref, *, nb, tq):
    nh = CONV_WIDTH - 1
    a = h3_ref[:, O_AVAL:O_AVAL + CONV_CH]
    gt = h3_ref[:, O_AGATE:O_AGATE + CONV_CH]
    u = a * _sigmoid(gt)
    for sl in range(N_SLAB):
        slab_ref[sl] = u[:, sl * LANES:(sl + 1) * LANES]
    hist_ref[0:nh] = st_ref[...]
    for t in range(tq):
        hist_ref[nh + t] = jnp.concatenate(
            [slab_ref[sl, pl.ds(t, nb, stride=tq), :] for sl in range(N_SLAB)], axis=1)

    def conv_step(t, carry):
        acc = jnp.zeros((nb, CONV_CH), F32)
        for j in range(CONV_WIDTH):
            wj = jnp.concatenate([cw_ref[j * 8:(j + 1) * 8, :]] * (nb // 8), axis=0)
            acc = acc + hist_ref[t + j] * wj
        cvt_ref[t] = acc + cb_ref[...]
        return carry

    lax.fori_loop(0, tq, conv_step, 0)
    for t in range(tq):
        cvt = cvt_ref[t]
        for sl in range(N_SLAB):
            slab_ref[sl, pl.ds(t, nb, stride=tq), :] = cvt[:, sl * LANES:(sl + 1) * LANES]
    cst_ref[...] = hist_ref[tq:tq + nh]

    cv = jnp.concatenate([slab_ref[sl] for sl in range(N_SLAB)], axis=1)
    y = _layer_norm(cv, lg_ref[...], lb_ref[...])
    ca = (_silu(y) * _silu(h3_ref[:, O_ZA:O_ZA + CONV_CH])).astype(BF16)
    gapa_ref[...] = (_sigmoid(hga_ref[...]) * _dot(ca, wa_ref[...])).astype(BF16)
    zb_ref[...] = _silu(hzb_ref[...]).astype(BF16)
    sgb_ref[...] = _sigmoid(hgb_ref[...]).astype(BF16)


def _sample_conv_branch(hs, state_t, cw8, conv_b, lg, lb, wa_bf, nb, tq):
    T = nb * tq
    nh = CONV_WIDTH - 1
    col = lambda width, off: pl.BlockSpec((T, width), lambda i: (0, off // width))
    full = lambda shape: pl.BlockSpec(shape, lambda i: (0,) * len(shape))
    return pl.pallas_call(
        functools.partial(_sample_conv_kernel, nb=nb, tq=tq),
        grid=(1,),
        in_specs=[
            col(3 * CONV_CH, 0), col(D_MODEL, O_GA), col(ATT_W, O_ZB), col(D_MODEL, O_GB),
            full((nh, nb, CONV_CH)), full((CONV_WIDTH * 8, CONV_CH)),
            full((1, CONV_CH)), full((1, CONV_CH)), full((1, CONV_CH)), full((CONV_CH, D_MODEL)),
        ],
        out_specs=[full((T, D_MODEL)), full((T, ATT_W)), full((T, D_MODEL)), full((nh, nb, CONV_CH))],
        out_shape=[
            jax.ShapeDtypeStruct((T, D_MODEL), BF16),
            jax.ShapeDtypeStruct((T, ATT_W), BF16),
            jax.ShapeDtypeStruct((T, D_MODEL), BF16),
            jax.ShapeDtypeStruct((nh, nb, CONV_CH), F32),
        ],
        scratch_shapes=[pltpu.VMEM((nh + tq, nb, CONV_CH), F32), pltpu.VMEM((N_SLAB, T, LANES), F32),
                        pltpu.VMEM((tq, nb, CONV_CH), F32)],
        compiler_params=pltpu.CompilerParams(dimension_semantics=("arbitrary",), vmem_limit_bytes=VMEM_LIMIT),
        name="sample_conv_branch",
    )(hs, hs, hs, hs, state_t, cw8, conv_b, lg, lb, wa_bf)


def _sample_bias_tables(g, tq):
    win, dil = GROUPS[g]
    slopes = _alibi_slopes()
    tabc = np.full((N_SLOTS * tq, win), NEG, np.float32)
    tabn = np.full((N_SLOTS * tq, tq), NEG, np.float32)
    pos = np.arange(win)
    for h in range(N_SLOTS):
        for t in range(tq):
            row = h * tq + t
            dist = win + t - pos
            ok = (dist % dil == 0) & (dist // dil <= win // dil)
            tabc[row, ok] = -slopes[h] * dist[ok].astype(np.float32)
            for t2 in range(t + 1):
                if (t - t2) % dil == 0:
                    tabn[row, t2] = -slopes[h] * np.float32(t - t2)
    return tabc, tabn


def _sample_attn_kernel(q_ref, k_ref, v_ref, c1_ref, c2_ref, c3_ref, tc1_ref, tc2_ref, tc3_ref,
                        tn1_ref, tn2_ref, tn3_ref, hm_ref, o_ref,
                        p1_ref, p2_ref, p3_ref, pn_ref, den_ref, *, tq):
    kv = pl.program_id(1)
    caches = (c1_ref, c2_ref, c3_ref)
    probs = (p1_ref, p2_ref, p3_ref)
    tabc = (tc1_ref, tc2_ref, tc3_ref)
    tabn = (tn1_ref, tn2_ref, tn3_ref)
    hm = hm_ref[...]

    @pl.when(kv == 0)
    def _():
        scores, mx = [], None
        for g in range(N_GROUPS):
            c0 = g * ATT_W
            qbd = (jnp.concatenate([q_ref[:, c0:c0 + ATT_W]] * N_SLOTS, axis=0) * hm).astype(BF16)
            sc = _dot(qbd, caches[g][...].astype(BF16)) + tabc[g][...]
            sn = _dot_nt(qbd, k_ref[:, c0:c0 + ATT_W].astype(BF16)) + tabn[g][...]
            m = jnp.maximum(jnp.max(sc, axis=1, keepdims=True), jnp.max(sn, axis=1, keepdims=True))
            mx = m if mx is None else jnp.maximum(mx, m)
            scores.append((sc, sn))
        den = jnp.zeros((N_SLOTS * tq, 1), F32)
        for g in range(N_GROUPS):
            sc, sn = scores[g]
            p = jnp.exp(sc - mx)
            pn = jnp.exp(sn - mx)
            den = den + jnp.sum(p, axis=1, keepdims=True) + jnp.sum(pn, axis=1, keepdims=True)
            probs[g][...] = p.astype(BF16)
            pn_ref[g] = pn
        den_ref[...] = den

    @pl.when(kv == 1)
    def _():
        acc = jnp.zeros((N_SLOTS * tq, ATT_W), F32)
        for g in range(N_GROUPS):
            c0 = g * ATT_W
            acc = acc + _dot_nt(probs[g][...], caches[g][...].astype(BF16))
            acc = acc + _dot(pn_ref[g].astype(BF16), v_ref[:, c0:c0 + ATT_W].astype(BF16))
        out = (acc / den_ref[...]) * hm
        o_ref[...] = jnp.sum(out.reshape(N_SLOTS, tq, ATT_W), axis=0)


def _sample_attn(hs3, caches_t, nb, tq):
    cspecs, tcs, tns = [], [], []
    for g, (win, _) in enumerate(GROUPS):
        cspecs.append(pl.BlockSpec((None, None, ATT_W, win), lambda b, kv: (b, kv, 0, 0)))
        tc, tn = _sample_bias_tables(g, tq)
        tcs.append(jnp.asarray(tc))
        tns.append(jnp.asarray(tn))
    hm = np.zeros((N_SLOTS * tq, ATT_W), np.float32)
    for h in range(N_SLOTS):
        hm[h * tq:(h + 1) * tq, h * HEAD_DIM:(h + 1) * HEAD_DIM] = 1.0
    qkv = lambda off: pl.BlockSpec((None, tq, QKV_W), lambda b, kv: (b, 0, off // QKV_W))
    const = lambda a: _const_spec(a.shape, (0,) * a.ndim)
    nrow = N_SLOTS * tq
    return pl.pallas_call(
        functools.partial(_sample_attn_kernel, tq=tq),
        grid=(nb, 2),
        in_specs=[qkv(O_Q), qkv(O_K), qkv(O_V)] + cspecs + [const(t) for t in tcs] + [const(t) for t in tns]
        + [_const_spec(hm.shape, (0, 0))],
        out_specs=pl.BlockSpec((None, tq, ATT_W), lambda b, kv: (b, 0, 0)),
        out_shape=jax.ShapeDtypeStruct((nb, tq, ATT_W), F32),
        scratch_shapes=[pltpu.VMEM((nrow, win), BF16) for win, _ in GROUPS]
        + [pltpu.VMEM((N_GROUPS, nrow, tq), F32), pltpu.VMEM((nrow, 1), F32)],
        compiler_params=pltpu.CompilerParams(
            dimension_semantics=("arbitrary", "arbitrary"), vmem_limit_bytes=VMEM_LIMIT),
        name="sample_attn",
    )(hs3, hs3, hs3, *caches_t, *tcs, *tns, jnp.asarray(hm))


def _to_feature_major(c):
    n, w = c.shape[0], c.shape[1]
    return jnp.transpose(c, (0, 2, 3, 4, 1)).reshape(n, 2, ATT_W, w)


def _from_feature_major(c):
    n, _, _, w = c.shape
    return jnp.transpose(c.reshape(n, 2, N_SLOTS, HEAD_DIM, w), (0, 4, 1, 2, 3))


def kernel(x_prompt, x_sample, cache_kv_w128, cache_kv_w512, cache_kv_w2048, state_conv, w_in, b_in, conv_w, conv_b,
           conv_ln_g, conv_ln_b, w_a, w_b, w_out, ln_g, ln_b):
    B, S, _ = x_prompt.shape
    nb, tq, _ = x_sample.shape
    caches = (cache_kv_w128, cache_kv_w512, cache_kv_w2048)
    tm = 512
    for (win, dil), c in zip(GROUPS, caches):
        assert c.shape[1] == win and S % (dil * Q_BLK) == 0 and S >= win and tm % (dil * 16) == 0
    assert tq == 8 and nb % 8 == 0 and state_conv.shape[1] == CONV_WIDTH - 1 and S % tm == 0

    col_scale = np.ones((IN_W,), np.float32)
    col_scale[O_Q:O_K] = HEAD_DIM ** -0.5
    wbf = (w_in * col_scale).astype(BF16)
    b1 = b_in * col_scale
    bsc = b1.reshape(1, IN_W)
    wkvt = lax.optimization_barrier(w_in[:, O_K:O_ZB]).T.astype(BF16)
    wkt, wvt = wkvt[:QKV_W], wkvt[QKV_W:]
    bkt, bvt = b1[O_K:O_V].reshape(QKV_W, 1), b1[O_V:O_ZB].reshape(QKV_W, 1)
    cw8 = jnp.broadcast_to(conv_w[:, None, :], (CONV_WIDTH, 8, CONV_CH)).reshape(CONV_WIDTH * 8, CONV_CH)
    cb, clg, clb = (t.reshape(1, CONV_CH) for t in (conv_b, conv_ln_g, conv_ln_b))
    lg, lb = ln_g.reshape(1, D_MODEL), ln_b.reshape(1, D_MODEL)
    wa_bf, wb_bf, wo_bf = w_a.astype(BF16), w_b.astype(BF16), w_out.astype(BF16)

    Ts = nb * tq
    hs = _sample_proj(x_sample.reshape(Ts, D_MODEL), wbf, bsc)
    hs3 = hs.reshape(nb, tq, IN_W)
    kv_new = jnp.stack([hs3[:, :, O_K:O_V], hs3[:, :, O_V:O_ZB]], axis=1)
    caches_t = [_to_feature_major(c) for c in caches]

    gapa, cst_p, *kvs_rows = _conv_branch_prompt(x_prompt, wbf, bsc, cw8, cb, clg, clb, wa_bf, kv_new, caches_t, tm)
    qkv1, qkv2, qkv3, zb, sgb = _qkv_prompt(x_prompt, wbf, bsc, tm)
    kvt = _kv_window(x_prompt, wkt, wvt, bkt, bvt, tm)
    parts = [_attn_prompt(t, g) for g, t in enumerate((qkv1, qkv2, qkv3))]
    y_p = _merge_groups(x_prompt, gapa, sgb, zb, parts, wb_bf, wo_bf, lg, lb, tm)
    kv_p = [_from_feature_major(t) for t in kvt]
    conv_p = cst_p[:, HOFF:, :]

    state_t = jnp.transpose(state_conv, (1, 0, 2))
    gapa_s, zb_s, sgb_s, cst_t = _sample_conv_branch(hs, state_t, cw8, cb, clg, clb, wa_bf, nb, tq)
    attn_s = _sample_attn(hs3, caches_t, nb, tq)
    y_s = _merge_plain(x_sample.reshape(Ts, D_MODEL), gapa_s, sgb_s, zb_s, attn_s.reshape(Ts, ATT_W),
                       wb_bf, wo_bf, lg, lb).reshape(nb, tq, D_MODEL)
    kv_s = [_from_feature_major(t.reshape(nb, 2, ATT_W, t.shape[-1])) for t in kvs_rows]
    conv_s = jnp.transpose(cst_t, (1, 0, 2))
    return (y_p, y_s, kv_p[0], kv_p[1], kv_p[2], conv_p, kv_s[0], kv_s[1], kv_s[2], conv_s)
```

```python
import functools

import numpy as np
import jax
import jax.numpy as jnp
from jax import lax
from jax.experimental import pallas as pl
from jax.experimental.pallas import tpu as pltpu

F32 = jnp.float32
BF16 = jnp.bfloat16

D_MODEL = 1024
CONV_CH = 1024
CONV_WIDTH = 31
N_SLOTS = 8
HEAD_DIM = 64
GROUPS = ((128, 1), (512, 4), (2048, 16))
N_GROUPS = len(GROUPS)
ATT_W = N_SLOTS * HEAD_DIM
QKV_W = N_GROUPS * ATT_W
Q_BLK = 128
DEPTH = 1
ALPHA = (2.0 * DEPTH) ** 0.25
LN_EPS = 1e-5
NEG = -1e30

O_AVAL, O_AGATE, O_ZA = 0, CONV_CH, 2 * CONV_CH
O_Q = 3 * CONV_CH
O_K = O_Q + QKV_W
O_V = O_K + QKV_W
O_ZB = O_V + QKV_W
O_GA = O_ZB + ATT_W
O_GB = O_GA + D_MODEL
IN_W = O_GB + D_MODEL

HIST = 32
HOFF = HIST - (CONV_WIDTH - 1)
LANES = 128
VMEM_LIMIT = 56 * 1024 * 1024


def _dot(a, b):
    return jnp.dot(a, b, preferred_element_type=F32)


def _dot_nt(a, b):
    return lax.dot_general(a, b, (((1,), (1,)), ((), ())), preferred_element_type=F32)


def _sigmoid(x):
    return 0.5 * jnp.tanh(0.5 * x) + 0.5


def _silu(x):
    h = 0.5 * x
    return h * jnp.tanh(h) + h


def _layer_norm(x, g, b):
    mu = jnp.mean(x, axis=-1, keepdims=True)
    xc = x - mu
    var = jnp.mean(xc * xc, axis=-1, keepdims=True)
    return xc * lax.rsqrt(var + LN_EPS) * g + b


def _const_spec(shape, index):
    return pl.BlockSpec(shape, lambda *_: index, pipeline_mode=pl.Buffered(1))


def _alibi_slopes():
    return (2.0 ** (-8.0 * np.arange(1, N_SLOTS + 1, dtype=np.float64) / N_SLOTS)).astype(np.float32)


SHIFT_ROWS = 64


def _stage_new_rows(new_ref, newt_ref, tq):
    for g in range(N_GROUPS):
        for i in range(new_ref.shape[0]):
            rows = new_ref[i, :, g * ATT_W:(g + 1) * ATT_W]
            padded = jnp.concatenate([jnp.zeros((LANES - tq, ATT_W), F32), rows], axis=0)
            newt_ref[g, i * ATT_W:(i + 1) * ATT_W, :] = padded.T


def _shift_cache_rows(c_ref, newt_ref, u_ref, g, r0, tq):
    win = GROUPS[g][0]
    lane = lax.broadcasted_iota(jnp.int32, (SHIFT_ROWS, LANES), 1)
    rows = pl.ds(r0, SHIFT_ROWS)
    rolled = pltpu.roll(c_ref[rows, :], win - tq, axis=1)
    if win > LANES:
        u_ref[rows, 0:win - LANES] = rolled[:, 0:win - LANES]
    u_ref[rows, win - LANES:] = jnp.where(lane >= LANES - tq, newt_ref[g, rows, :], rolled[:, win - LANES:])


def _conv_branch_kernel(x_ref, w3_ref, wga_ref, b3_ref, bga_ref, cw_ref, cb_ref, lg_ref, lb_ref, wa_ref,
                        new_ref, c1_ref, c2_ref, c3_ref,
                        gapa_ref, cst_ref, u1_ref, u2_ref, u3_ref, uh_ref, cv_ref, ca_ref, newt_ref,
                        *, tm, ns, tq):
    s = pl.program_id(1)
    ch = 256
    _stage_new_rows(new_ref, newt_ref, tq)

    @pl.when(s == 0)
    def _():
        uh_ref[0:HIST, :] = jnp.zeros((HIST, CONV_CH), F32)

    xb = x_ref[...].astype(BF16)

    def proj(c0, n):
        return _dot(xb, w3_ref[:, c0:c0 + n]) + b3_ref[:, c0:c0 + n]

    for c in range(0, CONV_CH, ch):
        a = proj(O_AVAL + c, ch)
        gt = proj(O_AGATE + c, ch)
        uh_ref[HIST:HIST + tm, c:c + ch] = a * _sigmoid(gt)

    rc = 64

    def conv_rows(i, carry):
        base = pl.multiple_of(HIST - 8 + i * rc, 8)
        for c in range(0, CONV_CH, ch):
            run = None
            for b in range(7, -1, -1):
                part = None
                for a in range((CONV_WIDTH - 1 - b) // 8 + 1):
                    j = CONV_WIDTH - 1 - (8 * a + b)
                    wj = jnp.concatenate([cw_ref[j * 8:(j + 1) * 8, c:c + ch]] * (rc // 8 + 1), axis=0)
                    term = uh_ref[pl.ds(base - 8 * a, rc + 8), c:c + ch] * wj
                    part = term if part is None else part + term
                run = part if run is None else part + pltpu.roll(run, 1, axis=0)
            cv_ref[pl.ds(pl.multiple_of(i * rc, rc), rc), c:c + ch] = run[8:, :] + cb_ref[:, c:c + ch]
        for g, (c_ref, u_ref) in enumerate(((c1_ref, u1_ref), (c2_ref, u2_ref), (c3_ref, u3_ref))):
            for pair in range(c_ref.shape[0] // ATT_W):
                r0 = pl.multiple_of(pair * ATT_W + i * SHIFT_ROWS, SHIFT_ROWS)
                _shift_cache_rows(c_ref, newt_ref, u_ref, g, r0, tq)
        return carry

    assert (tm // rc) * SHIFT_ROWS == ATT_W
    lax.fori_loop(0, tm // rc, conv_rows, 0)

    @pl.when(s == ns - 1)
    def _():
        cst_ref[...] = uh_ref[tm:tm + HIST, :]

    uh_ref[0:HIST, :] = uh_ref[tm:tm + HIST, :]

    x = cv_ref[...]
    mu = jnp.mean(x, axis=-1, keepdims=True)
    xc = x - mu
    var = jnp.mean(xc * xc, axis=-1, keepdims=True)
    rstd = lax.rsqrt(var + LN_EPS)
    for c in range(0, CONV_CH, ch):
        y = (cv_ref[:, c:c + ch] - mu) * rstd * lg_ref[:, c:c + ch] + lb_ref[:, c:c + ch]
        za = proj(O_ZA + c, ch)
        ca_ref[:, c:c + ch] = (_silu(y) * _silu(za)).astype(BF16)

    ca = ca_ref[...]
    for c in range(0, D_MODEL, ch):
        pa = _dot(ca, wa_ref[:, c:c + ch])
        ga = _dot(xb, wga_ref[:, c:c + ch]) + bga_ref[:, c:c + ch]
        gapa_ref[:, c:c + ch] = (_sigmoid(ga) * pa).astype(BF16)


def _conv_branch_prompt(x, wbf, bsc, cw8, conv_b, lg, lb, wa_bf, kv_new, caches_t, tm):
    B, S, _ = x.shape
    ns = S // tm
    nb, _, tq, _ = kv_new.shape
    pairs = 2 * nb
    assert pairs % (B * ns) == 0
    ppb = pairs // (B * ns)
    kern = functools.partial(_conv_branch_kernel, tm=tm, ns=ns, tq=tq)
    cache_specs = [pl.BlockSpec((ppb * ATT_W, win), lambda b, s: (b * ns + s, 0)) for win, _ in GROUPS]
    cache_rows = [c.reshape(pairs * ATT_W, c.shape[-1]) for c in caches_t]
    return pl.pallas_call(
        kern,
        grid=(B, ns),
        in_specs=[
            pl.BlockSpec((None, tm, D_MODEL), lambda b, s: (b, s, 0)),
            _const_spec((D_MODEL, 3 * CONV_CH), (0, 0)),
            _const_spec((D_MODEL, D_MODEL), (0, O_GA // D_MODEL)),
            _const_spec((1, 3 * CONV_CH), (0, 0)),
            _const_spec((1, D_MODEL), (0, O_GA // D_MODEL)),
            _const_spec((CONV_WIDTH * 8, CONV_CH), (0, 0)),
            _const_spec((1, CONV_CH), (0, 0)),
            _const_spec((1, CONV_CH), (0, 0)),
            _const_spec((1, CONV_CH), (0, 0)),
            _const_spec((CONV_CH, D_MODEL), (0, 0)),
            pl.BlockSpec((ppb, tq, QKV_W), lambda b, s: (b * ns + s, 0, 0)),
        ] + cache_specs,
        out_specs=[
            pl.BlockSpec((None, tm, D_MODEL), lambda b, s: (b, s, 0)),
            pl.BlockSpec((None, HIST, CONV_CH), lambda b, s: (b, 0, 0)),
        ] + cache_specs,
        out_shape=[
            jax.ShapeDtypeStruct((B, S, D_MODEL), BF16),
            jax.ShapeDtypeStruct((B, HIST, CONV_CH), F32),
        ] + [jax.ShapeDtypeStruct(c.shape, F32) for c in cache_rows],
        scratch_shapes=[
            pltpu.VMEM((HIST + tm, CONV_CH), F32),
            pltpu.VMEM((tm, CONV_CH), F32),
            pltpu.VMEM((tm, CONV_CH), BF16),
            pltpu.VMEM((N_GROUPS, ppb * ATT_W, LANES), F32),
        ],
        compiler_params=pltpu.CompilerParams(
            dimension_semantics=("arbitrary", "arbitrary"), vmem_limit_bytes=VMEM_LIMIT),
        name="conv_branch_prompt",
    )(x, wbf, wbf, bsc, bsc, cw8, conv_b, lg, lb, wa_bf, kv_new.reshape(pairs, tq, QKV_W), *cache_rows)


N_SLAB = D_MODEL // LANES


N_SAMPLE_IN = 13
N_SAMPLE_SCRATCH = 5


def _qkv_kernel(x_ref, wq_ref, wk_ref, wv_ref, wzb_ref, wgb_ref, bq_ref, bk_ref, bv_ref, bzb_ref, bgb_ref,
                *rest, tm, ns, tq):
    sample_in = rest[:N_SAMPLE_IN]
    o1_ref, o2_ref, o3_ref, zb_ref, sgb_ref, so_ref = rest[N_SAMPLE_IN:N_SAMPLE_IN + 6]
    xs_ref = rest[N_SAMPLE_IN + 6]
    sample_scratch = rest[N_SAMPLE_IN + 7:]
    step = pl.program_id(0) * ns + pl.program_id(1)
    _sample_attn_step(step % 2, *sample_in, so_ref, *sample_scratch, tq=tq)
    xb = x_ref[...].astype(BF16)
    for sl in range(N_SLAB):
        xs_ref[sl] = x_ref[:, sl * LANES:(sl + 1) * LANES]
    outs = (o1_ref, o2_ref, o3_ref)
    for g, (_, dil) in enumerate(GROUPS):
        n = tm // dil
        if dil == 1:
            lhs = xb
        else:
            lhs = jnp.concatenate(
                [jnp.concatenate([xs_ref[sl, pl.ds(r, n, stride=dil), :] for sl in range(N_SLAB)], axis=1)
                 for r in range(dil)], axis=0).astype(BF16)
        c0 = g * ATT_W
        for part, (w_ref, b_ref) in enumerate(((wq_ref, bq_ref), (wk_ref, bk_ref), (wv_ref, bv_ref))):
            res = _dot(lhs, w_ref[:, c0:c0 + ATT_W]) + b_ref[:, c0:c0 + ATT_W]
            outs[g][:, :, part * ATT_W:(part + 1) * ATT_W] = res.astype(BF16).reshape(dil, n, ATT_W)
    zb = _dot(xb, wzb_ref[...]) + bzb_ref[...]
    zb_ref[...] = _silu(zb).astype(BF16)
    for c in range(0, D_MODEL, 512):
        gb = _dot(xb, wgb_ref[:, c:c + 512]) + bgb_ref[:, c:c + 512]
        sgb_ref[:, c:c + 512] = _sigmoid(gb).astype(BF16)


def _qkv_prompt(x, wbf, bsc, hs3, caches_t, tm):
    B, S, _ = x.shape
    ns = S // tm
    nb, tq, _ = hs3.shape
    assert B * ns == 2 * nb
    s_in, s_args, s_out_spec, s_out_shape, s_scratch = _sample_attn_operands(
        hs3, caches_t, nb, tq, lambda b, s: b * ns + s)
    assert len(s_in) == N_SAMPLE_IN and len(s_scratch) == N_SAMPLE_SCRATCH
    tok = lambda width: pl.BlockSpec((None, tm, width), lambda b, s: (b, s, 0))
    grp = lambda dil: pl.BlockSpec((None, dil, tm // dil, QKV_W), lambda b, s: (b, 0, s, 0))
    wspec = lambda width, off: _const_spec((D_MODEL, width), (0, off // width))
    bspec = lambda width, off: _const_spec((1, width), (0, off // width))
    return pl.pallas_call(
        functools.partial(_qkv_kernel, tm=tm, ns=ns, tq=tq),
        grid=(B, ns),
        in_specs=[
            tok(D_MODEL),
            wspec(QKV_W, O_Q), wspec(QKV_W, O_K), wspec(QKV_W, O_V), wspec(ATT_W, O_ZB), wspec(D_MODEL, O_GB),
            bspec(QKV_W, O_Q), bspec(QKV_W, O_K), bspec(QKV_W, O_V), bspec(ATT_W, O_ZB), bspec(D_MODEL, O_GB),
        ] + s_in,
        out_specs=[grp(d) for _, d in GROUPS] + [tok(ATT_W), tok(D_MODEL), s_out_spec],
        out_shape=[jax.ShapeDtypeStruct((B, d, S // d, QKV_W), BF16) for _, d in GROUPS] + [
            jax.ShapeDtypeStruct((B, S, ATT_W), BF16),
            jax.ShapeDtypeStruct((B, S, D_MODEL), BF16),
            s_out_shape,
        ],
        scratch_shapes=[pltpu.VMEM((N_SLAB, tm, LANES), F32)] + s_scratch,
        compiler_params=pltpu.CompilerParams(
            dimension_semantics=("arbitrary", "arbitrary"), vmem_limit_bytes=VMEM_LIMIT),
        name="qkv_prompt",
    )(x, wbf, wbf, wbf, wbf, wbf, bsc, bsc, bsc, bsc, bsc, *s_args)


def _kv_window_kernel(x_ref, wkt_ref, wvt_ref, bkt_ref, bvt_ref, o1_ref, o2_ref, o3_ref, *, tw, nw, keeps):
    j = pl.program_id(1)
    xb = x_ref[...].astype(BF16)
    for g, o_ref in enumerate((o1_ref, o2_ref, o3_ref)):
        rows = min(keeps[g], tw)
        nwt = keeps[g] // rows
        c0 = g * ATT_W

        @pl.when(j >= nw - nwt)
        def _(o_ref=o_ref, rows=rows, c0=c0):
            xw = xb[tw - rows:, :]
            o_ref[0] = _dot_nt(wkt_ref[c0:c0 + ATT_W, :], xw) + bkt_ref[c0:c0 + ATT_W, :]
            o_ref[1] = _dot_nt(wvt_ref[c0:c0 + ATT_W, :], xw) + bvt_ref[c0:c0 + ATT_W, :]


def _kv_window(x, wkt, wvt, bkt, bvt, tw):
    B, S, _ = x.shape
    keeps = tuple(min(w, S) for w, _ in GROUPS)
    nw = max(keeps) // tw
    first = S // tw - nw

    def out_spec(keep):
        rows = min(keep, tw)
        f = nw - keep // rows
        return pl.BlockSpec((None, 2, ATT_W, rows), lambda b, j: (b, 0, 0, jnp.maximum(j - f, 0)))

    return pl.pallas_call(
        functools.partial(_kv_window_kernel, tw=tw, nw=nw, keeps=keeps),
        grid=(B, nw),
        in_specs=[
            pl.BlockSpec((None, tw, D_MODEL), lambda b, j: (b, first + j, 0)),
            _const_spec((QKV_W, D_MODEL), (0, 0)), _const_spec((QKV_W, D_MODEL), (0, 0)),
            _const_spec((QKV_W, 1), (0, 0)), _const_spec((QKV_W, 1), (0, 0)),
        ],
        out_specs=[out_spec(k) for k in keeps],
        out_shape=[jax.ShapeDtypeStruct((B, 2, ATT_W, k), F32) for k in keeps],
        compiler_params=pltpu.CompilerParams(
            dimension_semantics=("arbitrary", "arbitrary"), vmem_limit_bytes=VMEM_LIMIT),
        name="kv_window",
    )(x, wkt, wvt, bkt, bvt)


def _prompt_bias_tables(dil):
    slopes = _alibi_slopes()
    i = np.arange(Q_BLK)[:, None]
    j = np.arange(2 * Q_BLK)[None, :]
    delta = i + Q_BLK - j
    band = (delta >= 0) & (delta <= Q_BLK)
    bias = -slopes[:, None, None] * (delta * dil).astype(np.float32)[None]
    rest = np.where(band[None], bias, np.float32(NEG)).astype(np.float32)
    first = np.where((band & (j >= Q_BLK))[None], bias, np.float32(NEG)).astype(np.float32)
    return np.stack([first, rest]), rest


def _attn_kernel(q_ref, kp_ref, kc_ref, vp_ref, vc_ref, b0_ref, br_ref, o_ref, lse_ref, *, nsub, rpb):
    lane = lax.broadcasted_iota(jnp.int32, (Q_BLK, LANES), 1)
    lo = lane < HEAD_DIM
    m_lo = lo.astype(F32).astype(BF16)
    m_hi = (1.0 - lo.astype(F32)).astype(BF16)
    ones = jnp.ones((2 * Q_BLK, LANES), BF16)
    for rr, i in ((rr, i) for rr in range(rpb) for i in range(nsub)):
        r0 = i * Q_BLK
        lse_tile = jnp.zeros((Q_BLK, LANES), F32)
        for hp in range(N_SLOTS // 2):
            l0 = hp * LANES
            q2 = q_ref[rr, r0:r0 + Q_BLK, l0:l0 + LANES]
            if i == 0:
                k2 = jnp.concatenate([kp_ref[rr, :, l0:l0 + LANES], kc_ref[rr, 0:Q_BLK, l0:l0 + LANES]], axis=0)
                v2 = jnp.concatenate([vp_ref[rr, :, l0:l0 + LANES], vc_ref[rr, 0:Q_BLK, l0:l0 + LANES]], axis=0)
                bref = b0_ref
            else:
                k2 = kc_ref[rr, r0 - Q_BLK:r0 + Q_BLK, l0:l0 + LANES]
                v2 = vc_ref[rr, r0 - Q_BLK:r0 + Q_BLK, l0:l0 + LANES]
                bref = br_ref
            outs = []
            for par in range(2):
                h = 2 * hp + par
                qm = q2 * (m_lo if par == 0 else m_hi)
                sc = _dot_nt(qm, k2) + bref[h]
                mx = jnp.max(sc, axis=1, keepdims=True)
                p = jnp.exp(sc - mx).astype(BF16)
                oe = _dot(p, jnp.concatenate([v2, ones], axis=1))
                outs.append(oe[:, 0:LANES])
                lse_tile = jnp.where(lane == h, mx, lse_tile)
                lse_tile = jnp.where(lane == N_SLOTS + h, oe[:, LANES:], lse_tile)
            o_ref[rr, r0:r0 + Q_BLK, l0:l0 + LANES] = jnp.where(lo, outs[0], outs[1]).astype(BF16)
        lse_ref[rr, r0:r0 + Q_BLK, :] = lse_tile


ATTN_STEP_ROWS = 1024


def _attn_prompt(qkv, g):
    B, dil, L, _ = qkv.shape
    qb = min(ATTN_STEP_ROWS, L)
    nq = L // qb
    nsub = qb // Q_BLK
    rpb = min(dil, max(1, ATTN_STEP_ROWS // qb))
    b0, br = _prompt_bias_tables(dil)
    cur = lambda part: pl.BlockSpec((None, rpb, qb, ATT_W), lambda b, r, n: (b, r, n, part))
    prev = lambda part: pl.BlockSpec((None, rpb, Q_BLK, ATT_W),
                                     lambda b, r, n: (b, r, jnp.maximum(n * nsub - 1, 0), part))
    return pl.pallas_call(
        functools.partial(_attn_kernel, nsub=nsub, rpb=rpb),
        grid=(B, dil // rpb, nq),
        in_specs=[
            cur(0), prev(1), cur(1), prev(2), cur(2),
            pl.BlockSpec((None, N_SLOTS, Q_BLK, 2 * Q_BLK), lambda b, r, n: (jnp.minimum(n, 1), 0, 0, 0)),
            _const_spec((N_SLOTS, Q_BLK, 2 * Q_BLK), (0, 0, 0)),
        ],
        out_specs=[
            pl.BlockSpec((None, rpb, qb, ATT_W), lambda b, r, n: (b, r, n, 0)),
            pl.BlockSpec((None, rpb, qb, LANES), lambda b, r, n: (b, r, n, 0)),
        ],
        out_shape=[
            jax.ShapeDtypeStruct((B, dil, L, ATT_W), BF16),
            jax.ShapeDtypeStruct((B, dil, L, LANES), F32),
        ],
        compiler_params=pltpu.CompilerParams(
            dimension_semantics=("arbitrary", "arbitrary", "arbitrary"), vmem_limit_bytes=VMEM_LIMIT),
        name=f"attn_prompt_g{g}",
    )(qkv, qkv, qkv, qkv, qkv, jnp.asarray(b0), jnp.asarray(br))


def _merge_tail(x, gapa, sgb, zb, attn, wb_ref, wo_ref, g_ref, bt_ref):
    ab = (attn * zb.astype(F32)).astype(BF16)
    pb = _dot(ab, wb_ref[...])
    m = gapa.astype(F32) + sgb.astype(F32) * pb
    o = _dot(m.astype(BF16), wo_ref[...])
    return _layer_norm(ALPHA * x + o, g_ref[...], bt_ref[...])


def _merge_groups_kernel(x_ref, gapa_ref, sgb_ref, zb_ref, o1_ref, o2_ref, o3_ref, l1_ref, l2_ref, l3_ref,
                         ex_ref, wb_ref, wo_ref, g_ref, bt_ref, y_ref, nat_ref, natl_ref, *, tm):
    nsl = ATT_W // LANES
    for gi, (o_ref, l_ref) in enumerate(((o2_ref, l2_ref), (o3_ref, l3_ref))):
        dil = GROUPS[gi + 1][1]
        n = tm // dil
        for r in range(dil):
            blk = o_ref[r].astype(F32)
            for sl in range(nsl):
                nat_ref[gi, sl, pl.ds(r, n, stride=dil), :] = blk[:, sl * LANES:(sl + 1) * LANES]
            natl_ref[gi, pl.ds(r, n, stride=dil), :] = l_ref[r]
    sides = (l1_ref[...], natl_ref[0], natl_ref[1])
    mx = jnp.maximum(jnp.maximum(sides[0], sides[1]), sides[2])
    es = [jnp.exp(t - mx) for t in sides]
    dens = [pltpu.roll(t, LANES - N_SLOTS, axis=1) for t in sides]
    total = es[0] * dens[0] + es[1] * dens[1] + es[2] * dens[2]
    head_lane = lax.broadcasted_iota(jnp.int32, total.shape, 1) < N_SLOTS
    inv = jnp.where(head_lane, 1.0 / total, 0.0)
    ex = ex_ref[...]
    w1, w2, w3 = (_dot(jnp.where(head_lane, e * inv, 0.0).astype(BF16), ex) for e in es)
    slabs = []
    for sl in range(nsl):
        cs = slice(sl * LANES, (sl + 1) * LANES)
        slabs.append(w1[:, cs] * o1_ref[:, cs].astype(F32) + w2[:, cs] * nat_ref[0, sl] + w3[:, cs] * nat_ref[1, sl])
    attn = jnp.concatenate(slabs, axis=1)
    y_ref[...] = _merge_tail(x_ref[...], gapa_ref[...], sgb_ref[...], zb_ref[...], attn,
                             wb_ref, wo_ref, g_ref, bt_ref)


def _merge_plain_kernel(x_ref, gapa_ref, sgb_ref, zb_ref, attn_ref, wb_ref, wo_ref, g_ref, bt_ref, y_ref):
    y_ref[...] = _merge_tail(x_ref[...], gapa_ref[...], sgb_ref[...], zb_ref[...], attn_ref[...],
                             wb_ref, wo_ref, g_ref, bt_ref)


def _head_expand_matrix():
    ex = np.zeros((LANES, ATT_W), np.float32)
    for h in range(N_SLOTS):
        ex[h, h * HEAD_DIM:(h + 1) * HEAD_DIM] = 1.0
    return ex


def _merge_weight_specs():
    return [_const_spec((ATT_W, D_MODEL), (0, 0)), _const_spec((D_MODEL, D_MODEL), (0, 0)),
            _const_spec((1, D_MODEL), (0, 0)), _const_spec((1, D_MODEL), (0, 0))]


MERGE_ROWS = 1024


def _merge_groups(x, gapa, sgb, zb, parts, wb_bf, wo_bf, ln_g, ln_b, tm):
    B, S, _ = x.shape
    assert S % tm == 0 and tm % (GROUPS[-1][1] * 8) == 0
    tok = lambda width: pl.BlockSpec((None, tm, width), lambda b, s: (b, s, 0))
    grp = lambda dil, width: pl.BlockSpec((None, dil, tm // dil, width), lambda b, s: (b, 0, s, 0))
    (o1, l1), (o2, l2), (o3, l3) = parts
    d2, d3 = GROUPS[1][1], GROUPS[2][1]
    return pl.pallas_call(
        functools.partial(_merge_groups_kernel, tm=tm),
        grid=(B, S // tm),
        in_specs=[tok(D_MODEL), tok(D_MODEL), tok(D_MODEL), tok(ATT_W),
                  tok(ATT_W), grp(d2, ATT_W), grp(d3, ATT_W), tok(LANES), grp(d2, LANES), grp(d3, LANES),
                  _const_spec((LANES, ATT_W), (0, 0))] + _merge_weight_specs(),
        out_specs=tok(D_MODEL),
        out_shape=jax.ShapeDtypeStruct((B, S, D_MODEL), F32),
        scratch_shapes=[pltpu.VMEM((2, ATT_W // LANES, tm, LANES), F32), pltpu.VMEM((2, tm, LANES), F32)],
        compiler_params=pltpu.CompilerParams(
            dimension_semantics=("arbitrary", "arbitrary"), vmem_limit_bytes=VMEM_LIMIT),
        name="merge_groups",
    )(x, gapa, sgb, zb, o1.reshape(B, S, ATT_W), o2, o3, l1.reshape(B, S, LANES), l2, l3,
      jnp.asarray(_head_expand_matrix(), BF16), wb_bf, wo_bf, ln_g, ln_b)


def _merge_plain(x2, gapa, sgb, zb, attn, wb_bf, wo_bf, ln_g, ln_b):
    T = x2.shape[0]
    tok = lambda width: pl.BlockSpec((T, width), lambda t: (0, 0))
    return pl.pallas_call(
        _merge_plain_kernel,
        grid=(1,),
        in_specs=[tok(D_MODEL), tok(D_MODEL), tok(D_MODEL), tok(ATT_W), tok(ATT_W)] + _merge_weight_specs(),
        out_specs=tok(D_MODEL),
        out_shape=jax.ShapeDtypeStruct((T, D_MODEL), F32),
        compiler_params=pltpu.CompilerParams(dimension_semantics=("arbitrary",), vmem_limit_bytes=VMEM_LIMIT),
        name="merge_plain",
    )(x2, gapa, sgb, zb, attn, wb_bf, wo_bf, ln_g, ln_b)


def _sample_proj_kernel(x_ref, w_ref, b_ref, h_ref):
    h_ref[...] = _dot(x_ref[...].astype(BF16), w_ref[...]) + b_ref[...]


def _sample_proj(xs2, wbf, bsc):
    T = xs2.shape[0]
    cw = 1024
    return pl.pallas_call(
        _sample_proj_kernel,
        grid=(IN_W // cw,),
        in_specs=[
            pl.BlockSpec((T, D_MODEL), lambda j: (0, 0)),
            pl.BlockSpec((D_MODEL, cw), lambda j: (0, j)),
            pl.BlockSpec((1, cw), lambda j: (0, j)),
        ],
        out_specs=pl.BlockSpec((T, cw), lambda j: (0, j)),
        out_shape=jax.ShapeDtypeStruct((T, IN_W), F32),
        compiler_params=pltpu.CompilerParams(dimension_semantics=("arbitrary",), vmem_limit_bytes=VMEM_LIMIT),
        name="sample_proj",
    )(xs2, wbf, bsc)


def _sample_conv_kernel(h3_ref, hga_ref, hzb_ref, hgb_ref, st_ref, cw_ref, cb_ref, lg_ref, lb_ref, wa_ref,
                        gapa_ref, zb_ref, sgb_ref, cst_ref, hist_ref, slab_ref, cvt_ref, *, nb, tq):
    nh = CONV_WIDTH - 1
    a = h3_ref[:, O_AVAL:O_AVAL + CONV_CH]
    gt = h3_ref[:, O_AGATE:O_AGATE + CONV_CH]
    u = a * _sigmoid(gt)
    for sl in range(N_SLAB):
        slab_ref[sl] = u[:, sl * LANES:(sl + 1) * LANES]
    hist_ref[0:nh] = st_ref[...]
    for t in range(tq):
        hist_ref[nh + t] = jnp.concatenate(
            [slab_ref[sl, pl.ds(t, nb, stride=tq), :] for sl in range(N_SLAB)], axis=1)

    def conv_step(t, carry):
        acc = jnp.zeros((nb, CONV_CH), F32)
        for j in range(CONV_WIDTH):
            wj = jnp.concatenate([cw_ref[j * 8:(j + 1) * 8, :]] * (nb // 8), axis=0)
            acc = acc + hist_ref[t + j] * wj
        cvt_ref[t] = acc + cb_ref[...]
        return carry

    lax.fori_loop(0, tq, conv_step, 0)
    for t in range(tq):
        cvt = cvt_ref[t]
        for sl in range(N_SLAB):
            slab_ref[sl, pl.ds(t, nb, stride=tq), :] = cvt[:, sl * LANES:(sl + 1) * LANES]
    cst_ref[...] = hist_ref[tq:tq + nh]

    cv = jnp.concatenate([slab_ref[sl] for sl in range(N_SLAB)], axis=1)
    y = _layer_norm(cv, lg_ref[...], lb_ref[...])
    ca = (_silu(y) * _silu(h3_ref[:, O_ZA:O_ZA + CONV_CH])).astype(BF16)
    gapa_ref[...] = (_sigmoid(hga_ref[...]) * _dot(ca, wa_ref[...])).astype(BF16)
    zb_ref[...] = _silu(hzb_ref[...]).astype(BF16)
    sgb_ref[...] = _sigmoid(hgb_ref[...]).astype(BF16)


def _sample_conv_branch(hs, state_t, cw8, conv_b, lg, lb, wa_bf, nb, tq):
    T = nb * tq
    nh = CONV_WIDTH - 1
    col = lambda width, off: pl.BlockSpec((T, width), lambda i: (0, off // width))
    full = lambda shape: pl.BlockSpec(shape, lambda i: (0,) * len(shape))
    return pl.pallas_call(
        functools.partial(_sample_conv_kernel, nb=nb, tq=tq),
        grid=(1,),
        in_specs=[
            col(3 * CONV_CH, 0), col(D_MODEL, O_GA), col(ATT_W, O_ZB), col(D_MODEL, O_GB),
            full((nh, nb, CONV_CH)), full((CONV_WIDTH * 8, CONV_CH)),
            full((1, CONV_CH)), full((1, CONV_CH)), full((1, CONV_CH)), full((CONV_CH, D_MODEL)),
        ],
        out_specs=[full((T, D_MODEL)), full((T, ATT_W)), full((T, D_MODEL)), full((nh, nb, CONV_CH))],
        out_shape=[
            jax.ShapeDtypeStruct((T, D_MODEL), BF16),
            jax.ShapeDtypeStruct((T, ATT_W), BF16),
            jax.ShapeDtypeStruct((T, D_MODEL), BF16),
            jax.ShapeDtypeStruct((nh, nb, CONV_CH), F32),
        ],
        scratch_shapes=[pltpu.VMEM((nh + tq, nb, CONV_CH), F32), pltpu.VMEM((N_SLAB, T, LANES), F32),
                        pltpu.VMEM((tq, nb, CONV_CH), F32)],
        compiler_params=pltpu.CompilerParams(dimension_semantics=("arbitrary",), vmem_limit_bytes=VMEM_LIMIT),
        name="sample_conv_branch",
    )(hs, hs, hs, hs, state_t, cw8, conv_b, lg, lb, wa_bf)


def _sample_bias_tables(g, tq):
    win, dil = GROUPS[g]
    slopes = _alibi_slopes()
    tabc = np.full((N_SLOTS * tq, win), NEG, np.float32)
    tabn = np.full((N_SLOTS * tq, tq), NEG, np.float32)
    pos = np.arange(win)
    for h in range(N_SLOTS):
        for t in range(tq):
            row = h * tq + t
            dist = win + t - pos
            ok = (dist % dil == 0) & (dist // dil <= win // dil)
            tabc[row, ok] = -slopes[h] * dist[ok].astype(np.float32)
            for t2 in range(t + 1):
                if (t - t2) % dil == 0:
                    tabn[row, t2] = -slopes[h] * np.float32(t - t2)
    return tabc, tabn


def _sample_attn_step(kv, q_ref, k_ref, v_ref, c1_ref, c2_ref, c3_ref, tc1_ref, tc2_ref, tc3_ref,
                      tn1_ref, tn2_ref, tn3_ref, hm_ref, o_ref,
                      p1_ref, p2_ref, p3_ref, pn_ref, den_ref, *, tq):
    caches = (c1_ref, c2_ref, c3_ref)
    probs = (p1_ref, p2_ref, p3_ref)
    tabc = (tc1_ref, tc2_ref, tc3_ref)
    tabn = (tn1_ref, tn2_ref, tn3_ref)
    hm = hm_ref[...]

    @pl.when(kv == 0)
    def _():
        scores, mx = [], None
        for g in range(N_GROUPS):
            c0 = g * ATT_W
            qbd = (jnp.concatenate([q_ref[:, c0:c0 + ATT_W]] * N_SLOTS, axis=0) * hm).astype(BF16)
            sc = _dot(qbd, caches[g][...].astype(BF16)) + tabc[g][...]
            sn = _dot_nt(qbd, k_ref[:, c0:c0 + ATT_W].astype(BF16)) + tabn[g][...]
            m = jnp.maximum(jnp.max(sc, axis=1, keepdims=True), jnp.max(sn, axis=1, keepdims=True))
            mx = m if mx is None else jnp.maximum(mx, m)
            scores.append((sc, sn))
        den = jnp.zeros((N_SLOTS * tq, 1), F32)
        for g in range(N_GROUPS):
            sc, sn = scores[g]
            p = jnp.exp(sc - mx)
            pn = jnp.exp(sn - mx)
            den = den + jnp.sum(p, axis=1, keepdims=True) + jnp.sum(pn, axis=1, keepdims=True)
            probs[g][...] = p.astype(BF16)
            pn_ref[g] = pn
        den_ref[...] = den

    @pl.when(kv == 1)
    def _():
        acc = jnp.zeros((N_SLOTS * tq, ATT_W), F32)
        for g in range(N_GROUPS):
            c0 = g * ATT_W
            acc = acc + _dot_nt(probs[g][...], caches[g][...].astype(BF16))
            acc = acc + _dot(pn_ref[g].astype(BF16), v_ref[:, c0:c0 + ATT_W].astype(BF16))
        out = (acc / den_ref[...]) * hm
        o_ref[...] = jnp.sum(out.reshape(N_SLOTS, tq, ATT_W), axis=0)


def _sample_attn_operands(hs3, caches_t, nb, tq, step_of):
    seq = lambda *ids: step_of(*ids) // 2
    half = lambda *ids: step_of(*ids) % 2
    cspecs, tcs, tns = [], [], []
    for g, (win, _) in enumerate(GROUPS):
        cspecs.append(pl.BlockSpec((None, None, ATT_W, win), lambda *ids: (seq(*ids), half(*ids), 0, 0)))
        tc, tn = _sample_bias_tables(g, tq)
        tcs.append(jnp.asarray(tc))
        tns.append(jnp.asarray(tn))
    hm = np.zeros((N_SLOTS * tq, ATT_W), np.float32)
    for h in range(N_SLOTS):
        hm[h * tq:(h + 1) * tq, h * HEAD_DIM:(h + 1) * HEAD_DIM] = 1.0
    qkv = lambda off: pl.BlockSpec((None, tq, QKV_W), lambda *ids: (seq(*ids), 0, off // QKV_W))
    const = lambda a: _const_spec(a.shape, (0,) * a.ndim)
    nrow = N_SLOTS * tq
    in_specs = ([qkv(O_Q), qkv(O_K), qkv(O_V)] + cspecs + [const(t) for t in tcs] + [const(t) for t in tns]
                + [_const_spec(hm.shape, (0, 0))])
    args = [hs3, hs3, hs3, *caches_t, *tcs, *tns, jnp.asarray(hm)]
    out_spec = pl.BlockSpec((None, tq, ATT_W), lambda *ids: (seq(*ids), 0, 0))
    out_shape = jax.ShapeDtypeStruct((nb, tq, ATT_W), F32)
    scratch = ([pltpu.VMEM((nrow, win), BF16) for win, _ in GROUPS]
               + [pltpu.VMEM((N_GROUPS, nrow, tq), F32), pltpu.VMEM((nrow, 1), F32)])
    return in_specs, args, out_spec, out_shape, scratch


def _to_feature_major(c):
    n, w = c.shape[0], c.shape[1]
    return jnp.transpose(c, (0, 2, 3, 4, 1)).reshape(n, 2, ATT_W, w)


def _from_feature_major(c):
    n, _, _, w = c.shape
    return jnp.transpose(c.reshape(n, 2, N_SLOTS, HEAD_DIM, w), (0, 4, 1, 2, 3))


def kernel(x_prompt, x_sample, cache_kv_w128, cache_kv_w512, cache_kv_w2048, state_conv, w_in, b_in, conv_w, conv_b,
           conv_ln_g, conv_ln_b, w_a, w_b, w_out, ln_g, ln_b):
    B, S, _ = x_prompt.shape
    nb, tq, _ = x_sample.shape
    caches = (cache_kv_w128, cache_kv_w512, cache_kv_w2048)
    tm = 512
    for (win, dil), c in zip(GROUPS, caches):
        assert c.shape[1] == win and S % (dil * Q_BLK) == 0 and S >= win and tm % (dil * 16) == 0
    assert tq == 8 and nb % 8 == 0 and state_conv.shape[1] == CONV_WIDTH - 1 and S % tm == 0

    col_scale = np.ones((IN_W,), np.float32)
    col_scale[O_Q:O_K] = HEAD_DIM ** -0.5
    wbf = (w_in * col_scale).astype(BF16)
    b1 = b_in * col_scale
    bsc = b1.reshape(1, IN_W)
    wkvt = lax.optimization_barrier(w_in[:, O_K:O_ZB]).T.astype(BF16)
    wkt, wvt = wkvt[:QKV_W], wkvt[QKV_W:]
    bkt, bvt = b1[O_K:O_V].reshape(QKV_W, 1), b1[O_V:O_ZB].reshape(QKV_W, 1)
    cw8 = jnp.broadcast_to(conv_w[:, None, :], (CONV_WIDTH, 8, CONV_CH)).reshape(CONV_WIDTH * 8, CONV_CH)
    cb, clg, clb = (t.reshape(1, CONV_CH) for t in (conv_b, conv_ln_g, conv_ln_b))
    lg, lb = ln_g.reshape(1, D_MODEL), ln_b.reshape(1, D_MODEL)
    wa_bf, wb_bf, wo_bf = w_a.astype(BF16), w_b.astype(BF16), w_out.astype(BF16)

    Ts = nb * tq
    hs = _sample_proj(x_sample.reshape(Ts, D_MODEL), wbf, bsc)
    hs3 = hs.reshape(nb, tq, IN_W)
    kv_new = jnp.stack([hs3[:, :, O_K:O_V], hs3[:, :, O_V:O_ZB]], axis=1)
    caches_t = [_to_feature_major(c) for c in caches]

    gapa, cst_p, *kvs_rows = _conv_branch_prompt(x_prompt, wbf, bsc, cw8, cb, clg, clb, wa_bf, kv_new, caches_t, tm)
    qkv1, qkv2, qkv3, zb, sgb, attn_s = _qkv_prompt(x_prompt, wbf, bsc, hs3, caches_t, tm)
    kvt = _kv_window(x_prompt, wkt, wvt, bkt, bvt, tm)
    parts = [_attn_prompt(t, g) for g, t in enumerate((qkv1, qkv2, qkv3))]
    y_p = _merge_groups(x_prompt, gapa, sgb, zb, parts, wb_bf, wo_bf, lg, lb, MERGE_ROWS)
    kv_p = [_from_feature_major(t) for t in kvt]
    conv_p = cst_p[:, HOFF:, :]

    state_t = jnp.transpose(state_conv, (1, 0, 2))
    gapa_s, zb_s, sgb_s, cst_t = _sample_conv_branch(hs, state_t, cw8, cb, clg, clb, wa_bf, nb, tq)
    y_s = _merge_plain(x_sample.reshape(Ts, D_MODEL), gapa_s, sgb_s, zb_s, attn_s.reshape(Ts, ATT_W),
                       wb_bf, wo_bf, lg, lb).reshape(nb, tq, D_MODEL)
    kv_s = [_from_feature_major(t.reshape(nb, 2, ATT_W, t.shape[-1])) for t in kvs_rows]
    conv_s = jnp.transpose(cst_t, (1, 0, 2))
    return (y_p, y_s, kv_p[0], kv_p[1], kv_p[2], conv_p, kv_s[0], kv_s[1], kv_s[2], conv_s)
```

```python
import functools

import numpy as np
import jax
import jax.numpy as jnp
from jax import lax
from jax.experimental import pallas as pl
from jax.experimental.pallas import tpu as pltpu

F32 = jnp.float32
BF16 = jnp.bfloat16

D_MODEL = 1024
CONV_CH = 1024
CONV_WIDTH = 31
N_SLOTS = 8
HEAD_DIM = 64
GROUPS = ((128, 1), (512, 4), (2048, 16))
N_GROUPS = len(GROUPS)
ATT_W = N_SLOTS * HEAD_DIM
QKV_W = N_GROUPS * ATT_W
Q_BLK = 128
DEPTH = 1
ALPHA = (2.0 * DEPTH) ** 0.25
LN_EPS = 1e-5
NEG = -1e30

O_AVAL, O_AGATE, O_ZA = 0, CONV_CH, 2 * CONV_CH
O_Q = 3 * CONV_CH
O_K = O_Q + QKV_W
O_V = O_K + QKV_W
O_ZB = O_V + QKV_W
O_GA = O_ZB + ATT_W
O_GB = O_GA + D_MODEL
IN_W = O_GB + D_MODEL

HIST = 32
HOFF = HIST - (CONV_WIDTH - 1)
LANES = 128
VMEM_LIMIT = 56 * 1024 * 1024
PROMPT_ROWS = 512
PROJ_COLS = 256
CONV_ROWS = 64
GATE_COLS = 512
SAMPLE_PROJ_COLS = 2048


def _dot(a, b):
    return jnp.dot(a, b, preferred_element_type=F32)


def _dot_nt(a, b):
    return lax.dot_general(a, b, (((1,), (1,)), ((), ())), preferred_element_type=F32)


def _sigmoid(x):
    return 0.5 * jnp.tanh(0.5 * x) + 0.5


def _silu(x):
    h = 0.5 * x
    return h * jnp.tanh(h) + h


def _layer_norm(x, g, b):
    mu = jnp.mean(x, axis=-1, keepdims=True)
    xc = x - mu
    var = jnp.mean(xc * xc, axis=-1, keepdims=True)
    return xc * lax.rsqrt(var + LN_EPS) * g + b


def _const_spec(shape, index):
    return pl.BlockSpec(shape, lambda *_: index, pipeline_mode=pl.Buffered(1))


def _alibi_slopes():
    return (2.0 ** (-8.0 * np.arange(1, N_SLOTS + 1, dtype=np.float64) / N_SLOTS)).astype(np.float32)


SHIFT_ROWS = 64


def _stage_new_rows(new_ref, newt_ref, tq):
    for g in range(N_GROUPS):
        for i in range(new_ref.shape[0]):
            rows = new_ref[i, :, g * ATT_W:(g + 1) * ATT_W]
            padded = jnp.concatenate([jnp.zeros((LANES - tq, ATT_W), F32), rows], axis=0)
            newt_ref[g, i * ATT_W:(i + 1) * ATT_W, :] = padded.T


def _shift_cache_rows(c_ref, newt_ref, u_ref, g, r0, tq):
    win = GROUPS[g][0]
    lane = lax.broadcasted_iota(jnp.int32, (SHIFT_ROWS, LANES), 1)
    rows = pl.ds(r0, SHIFT_ROWS)
    rolled = pltpu.roll(c_ref[rows, :], win - tq, axis=1)
    if win > LANES:
        u_ref[rows, 0:win - LANES] = rolled[:, 0:win - LANES]
    u_ref[rows, win - LANES:] = jnp.where(lane >= LANES - tq, newt_ref[g, rows, :], rolled[:, win - LANES:])


def _conv_branch_kernel(x_ref, w3_ref, wga_ref, b3_ref, bga_ref, cw_ref, cb_ref, lg_ref, lb_ref, wa_ref,
                        new_ref, c1_ref, c2_ref, c3_ref,
                        gapa_ref, cst_ref, u1_ref, u2_ref, u3_ref, uh_ref, cv_ref, ca_ref, newt_ref,
                        *, tm, ns, tq):
    s = pl.program_id(1)
    ch = PROJ_COLS
    _stage_new_rows(new_ref, newt_ref, tq)

    @pl.when(s == 0)
    def _():
        uh_ref[0:HIST, :] = jnp.zeros((HIST, CONV_CH), F32)

    xb = x_ref[...].astype(BF16)

    def proj(c0, n):
        return _dot(xb, w3_ref[:, c0:c0 + n]) + b3_ref[:, c0:c0 + n]

    for c in range(0, CONV_CH, ch):
        a = proj(O_AVAL + c, ch)
        gt = proj(O_AGATE + c, ch)
        uh_ref[HIST:HIST + tm, c:c + ch] = a * _sigmoid(gt)

    rc = CONV_ROWS

    def conv_rows(i, carry):
        base = pl.multiple_of(HIST - 8 + i * rc, 8)
        for c in range(0, CONV_CH, ch):
            run = None
            for b in range(7, -1, -1):
                part = None
                for a in range((CONV_WIDTH - 1 - b) // 8 + 1):
                    j = CONV_WIDTH - 1 - (8 * a + b)
                    wj = jnp.concatenate([cw_ref[j * 8:(j + 1) * 8, c:c + ch]] * (rc // 8 + 1), axis=0)
                    term = uh_ref[pl.ds(base - 8 * a, rc + 8), c:c + ch] * wj
                    part = term if part is None else part + term
                run = part if run is None else part + pltpu.roll(run, 1, axis=0)
            cv_ref[pl.ds(pl.multiple_of(i * rc, rc), rc), c:c + ch] = run[8:, :] + cb_ref[:, c:c + ch]
        for g, (c_ref, u_ref) in enumerate(((c1_ref, u1_ref), (c2_ref, u2_ref), (c3_ref, u3_ref))):
            for pair in range(c_ref.shape[0] // ATT_W):
                r0 = pl.multiple_of(pair * ATT_W + i * SHIFT_ROWS, SHIFT_ROWS)
                _shift_cache_rows(c_ref, newt_ref, u_ref, g, r0, tq)
        return carry

    assert (tm // rc) * SHIFT_ROWS == ATT_W
    lax.fori_loop(0, tm // rc, conv_rows, 0)

    @pl.when(s == ns - 1)
    def _():
        cst_ref[...] = uh_ref[tm:tm + HIST, :]

    uh_ref[0:HIST, :] = uh_ref[tm:tm + HIST, :]

    x = cv_ref[...]
    mu = jnp.mean(x, axis=-1, keepdims=True)
    xc = x - mu
    var = jnp.mean(xc * xc, axis=-1, keepdims=True)
    rstd = lax.rsqrt(var + LN_EPS)
    for c in range(0, CONV_CH, ch):
        y = (cv_ref[:, c:c + ch] - mu) * rstd * lg_ref[:, c:c + ch] + lb_ref[:, c:c + ch]
        za = proj(O_ZA + c, ch)
        ca_ref[:, c:c + ch] = (_silu(y) * _silu(za)).astype(BF16)

    ca = ca_ref[...]
    for c in range(0, D_MODEL, ch):
        pa = _dot(ca, wa_ref[:, c:c + ch])
        ga = _dot(xb, wga_ref[:, c:c + ch]) + bga_ref[:, c:c + ch]
        gapa_ref[:, c:c + ch] = (_sigmoid(ga) * pa).astype(BF16)


def _conv_branch_prompt(x, wbf, bsc, cw8, conv_b, lg, lb, wa_bf, kv_new, caches_t, tm):
    B, S, _ = x.shape
    ns = S // tm
    nb, _, tq, _ = kv_new.shape
    pairs = 2 * nb
    assert pairs % (B * ns) == 0
    ppb = pairs // (B * ns)
    kern = functools.partial(_conv_branch_kernel, tm=tm, ns=ns, tq=tq)
    cache_specs = [pl.BlockSpec((ppb * ATT_W, win), lambda b, s: (b * ns + s, 0)) for win, _ in GROUPS]
    cache_rows = [c.reshape(pairs * ATT_W, c.shape[-1]) for c in caches_t]
    return pl.pallas_call(
        kern,
        grid=(B, ns),
        in_specs=[
            pl.BlockSpec((None, tm, D_MODEL), lambda b, s: (b, s, 0)),
            _const_spec((D_MODEL, 3 * CONV_CH), (0, 0)),
            _const_spec((D_MODEL, D_MODEL), (0, O_GA // D_MODEL)),
            _const_spec((1, 3 * CONV_CH), (0, 0)),
            _const_spec((1, D_MODEL), (0, O_GA // D_MODEL)),
            _const_spec((CONV_WIDTH * 8, CONV_CH), (0, 0)),
            _const_spec((1, CONV_CH), (0, 0)),
            _const_spec((1, CONV_CH), (0, 0)),
            _const_spec((1, CONV_CH), (0, 0)),
            _const_spec((CONV_CH, D_MODEL), (0, 0)),
            pl.BlockSpec((ppb, tq, QKV_W), lambda b, s: (b * ns + s, 0, 0)),
        ] + cache_specs,
        out_specs=[
            pl.BlockSpec((None, tm, D_MODEL), lambda b, s: (b, s, 0)),
            pl.BlockSpec((None, HIST, CONV_CH), lambda b, s: (b, 0, 0)),
        ] + cache_specs,
        out_shape=[
            jax.ShapeDtypeStruct((B, S, D_MODEL), BF16),
            jax.ShapeDtypeStruct((B, HIST, CONV_CH), F32),
        ] + [jax.ShapeDtypeStruct(c.shape, F32) for c in cache_rows],
        scratch_shapes=[
            pltpu.VMEM((HIST + tm, CONV_CH), F32),
            pltpu.VMEM((tm, CONV_CH), F32),
            pltpu.VMEM((tm, CONV_CH), BF16),
            pltpu.VMEM((N_GROUPS, ppb * ATT_W, LANES), F32),
        ],
        compiler_params=pltpu.CompilerParams(
            dimension_semantics=("arbitrary", "arbitrary"), vmem_limit_bytes=VMEM_LIMIT),
        name="conv_branch_prompt",
    )(x, wbf, wbf, bsc, bsc, cw8, conv_b, lg, lb, wa_bf, kv_new.reshape(pairs, tq, QKV_W), *cache_rows)


N_SLAB = D_MODEL // LANES


N_SAMPLE_IN = 13
N_SAMPLE_SCRATCH = 5


def _qkv_kernel(x_ref, wq_ref, wk_ref, wv_ref, wzb_ref, wgb_ref, bq_ref, bk_ref, bv_ref, bzb_ref, bgb_ref,
                *rest, tm, ns, tq):
    sample_in = rest[:N_SAMPLE_IN]
    o1_ref, o2_ref, o3_ref, zb_ref, sgb_ref, so_ref = rest[N_SAMPLE_IN:N_SAMPLE_IN + 6]
    xs_ref = rest[N_SAMPLE_IN + 6]
    sample_scratch = rest[N_SAMPLE_IN + 7:]
    step = pl.program_id(0) * ns + pl.program_id(1)
    _sample_attn_step(step % 2, *sample_in, so_ref, *sample_scratch, tq=tq)
    xb = x_ref[...].astype(BF16)
    for sl in range(N_SLAB):
        xs_ref[sl] = x_ref[:, sl * LANES:(sl + 1) * LANES]
    outs = (o1_ref, o2_ref, o3_ref)
    for g, (_, dil) in enumerate(GROUPS):
        n = tm // dil
        if dil == 1:
            lhs = xb
        else:
            lhs = jnp.concatenate(
                [jnp.concatenate([xs_ref[sl, pl.ds(r, n, stride=dil), :] for sl in range(N_SLAB)], axis=1)
                 for r in range(dil)], axis=0).astype(BF16)
        c0 = g * ATT_W
        for part, (w_ref, b_ref) in enumerate(((wq_ref, bq_ref), (wk_ref, bk_ref), (wv_ref, bv_ref))):
            res = _dot(lhs, w_ref[:, c0:c0 + ATT_W]) + b_ref[:, c0:c0 + ATT_W]
            outs[g][:, :, part * ATT_W:(part + 1) * ATT_W] = res.astype(BF16).reshape(dil, n, ATT_W)
    zb = _dot(xb, wzb_ref[...]) + bzb_ref[...]
    zb_ref[...] = _silu(zb).astype(BF16)
    for c in range(0, D_MODEL, GATE_COLS):
        gb = _dot(xb, wgb_ref[:, c:c + GATE_COLS]) + bgb_ref[:, c:c + GATE_COLS]
        sgb_ref[:, c:c + GATE_COLS] = _sigmoid(gb).astype(BF16)


def _qkv_prompt(x, wbf, bsc, hs3, caches_t, tm):
    B, S, _ = x.shape
    ns = S // tm
    nb, tq, _ = hs3.shape
    assert B * ns == 2 * nb
    s_in, s_args, s_out_spec, s_out_shape, s_scratch = _sample_attn_operands(
        hs3, caches_t, nb, tq, lambda b, s: b * ns + s)
    assert len(s_in) == N_SAMPLE_IN and len(s_scratch) == N_SAMPLE_SCRATCH
    tok = lambda width: pl.BlockSpec((None, tm, width), lambda b, s: (b, s, 0))
    grp = lambda dil: pl.BlockSpec((None, dil, tm // dil, QKV_W), lambda b, s: (b, 0, s, 0))
    wspec = lambda width, off: _const_spec((D_MODEL, width), (0, off // width))
    bspec = lambda width, off: _const_spec((1, width), (0, off // width))
    return pl.pallas_call(
        functools.partial(_qkv_kernel, tm=tm, ns=ns, tq=tq),
        grid=(B, ns),
        in_specs=[
            tok(D_MODEL),
            wspec(QKV_W, O_Q), wspec(QKV_W, O_K), wspec(QKV_W, O_V), wspec(ATT_W, O_ZB), wspec(D_MODEL, O_GB),
            bspec(QKV_W, O_Q), bspec(QKV_W, O_K), bspec(QKV_W, O_V), bspec(ATT_W, O_ZB), bspec(D_MODEL, O_GB),
        ] + s_in,
        out_specs=[grp(d) for _, d in GROUPS] + [tok(ATT_W), tok(D_MODEL), s_out_spec],
        out_shape=[jax.ShapeDtypeStruct((B, d, S // d, QKV_W), BF16) for _, d in GROUPS] + [
            jax.ShapeDtypeStruct((B, S, ATT_W), BF16),
            jax.ShapeDtypeStruct((B, S, D_MODEL), BF16),
            s_out_shape,
        ],
        scratch_shapes=[pltpu.VMEM((N_SLAB, tm, LANES), F32)] + s_scratch,
        compiler_params=pltpu.CompilerParams(
            dimension_semantics=("arbitrary", "arbitrary"), vmem_limit_bytes=VMEM_LIMIT),
        name="qkv_prompt",
    )(x, wbf, wbf, wbf, wbf, wbf, bsc, bsc, bsc, bsc, bsc, *s_args)


def _kv_window_kernel(x_ref, wkt_ref, wvt_ref, bkt_ref, bvt_ref, o1_ref, o2_ref, o3_ref, *, tw, nw, keeps):
    j = pl.program_id(1)
    xb = x_ref[...].astype(BF16)
    for g, o_ref in enumerate((o1_ref, o2_ref, o3_ref)):
        rows = min(keeps[g], tw)
        nwt = keeps[g] // rows
        c0 = g * ATT_W

        @pl.when(j >= nw - nwt)
        def _(o_ref=o_ref, rows=rows, c0=c0):
            xw = xb[tw - rows:, :]
            o_ref[0] = _dot_nt(wkt_ref[c0:c0 + ATT_W, :], xw) + bkt_ref[c0:c0 + ATT_W, :]
            o_ref[1] = _dot_nt(wvt_ref[c0:c0 + ATT_W, :], xw) + bvt_ref[c0:c0 + ATT_W, :]


def _kv_window(x, wkt, wvt, bkt, bvt, tw):
    B, S, _ = x.shape
    keeps = tuple(min(w, S) for w, _ in GROUPS)
    nw = max(keeps) // tw
    first = S // tw - nw

    def out_spec(keep):
        rows = min(keep, tw)
        f = nw - keep // rows
        return pl.BlockSpec((None, 2, ATT_W, rows), lambda b, j: (b, 0, 0, jnp.maximum(j - f, 0)))

    return pl.pallas_call(
        functools.partial(_kv_window_kernel, tw=tw, nw=nw, keeps=keeps),
        grid=(B, nw),
        in_specs=[
            pl.BlockSpec((None, tw, D_MODEL), lambda b, j: (b, first + j, 0)),
            _const_spec((QKV_W, D_MODEL), (0, 0)), _const_spec((QKV_W, D_MODEL), (0, 0)),
            _const_spec((QKV_W, 1), (0, 0)), _const_spec((QKV_W, 1), (0, 0)),
        ],
        out_specs=[out_spec(k) for k in keeps],
        out_shape=[jax.ShapeDtypeStruct((B, 2, ATT_W, k), F32) for k in keeps],
        compiler_params=pltpu.CompilerParams(
            dimension_semantics=("arbitrary", "arbitrary"), vmem_limit_bytes=VMEM_LIMIT),
        name="kv_window",
    )(x, wkt, wvt, bkt, bvt)


def _prompt_bias_tables(dil):
    slopes = _alibi_slopes()
    i = np.arange(Q_BLK)[:, None]
    j = np.arange(2 * Q_BLK)[None, :]
    delta = i + Q_BLK - j
    band = (delta >= 0) & (delta <= Q_BLK)
    bias = -slopes[:, None, None] * (delta * dil).astype(np.float32)[None]
    rest = np.where(band[None], bias, np.float32(NEG)).astype(np.float32)
    first = np.where((band & (j >= Q_BLK))[None], bias, np.float32(NEG)).astype(np.float32)
    return np.stack([first, rest]), rest


def _attn_kernel(q_ref, kp_ref, kc_ref, vp_ref, vc_ref, b0_ref, br_ref, o_ref, side_ref, *, nsub, rpb):
    lane = lax.broadcasted_iota(jnp.int32, (Q_BLK, LANES), 1)
    lo = lane < HEAD_DIM
    m_lo = lo.astype(F32).astype(BF16)
    m_hi = (1.0 - lo.astype(F32)).astype(BF16)
    ones = jnp.ones((2 * Q_BLK, LANES), BF16)
    for rr, i in ((rr, i) for rr in range(rpb) for i in range(nsub)):
        r0 = i * Q_BLK
        side = jnp.zeros((Q_BLK, LANES), F32)
        for hp in range(N_SLOTS // 2):
            l0 = hp * LANES
            q2 = q_ref[rr, r0:r0 + Q_BLK, l0:l0 + LANES]
            if i == 0:
                k2 = jnp.concatenate([kp_ref[rr, :, l0:l0 + LANES], kc_ref[rr, 0:Q_BLK, l0:l0 + LANES]], axis=0)
                v2 = jnp.concatenate([vp_ref[rr, :, l0:l0 + LANES], vc_ref[rr, 0:Q_BLK, l0:l0 + LANES]], axis=0)
                bref = b0_ref
            else:
                k2 = kc_ref[rr, r0 - Q_BLK:r0 + Q_BLK, l0:l0 + LANES]
                v2 = vc_ref[rr, r0 - Q_BLK:r0 + Q_BLK, l0:l0 + LANES]
                bref = br_ref
            outs = []
            for par in range(2):
                h = 2 * hp + par
                qm = q2 * (m_lo if par == 0 else m_hi)
                sc = _dot_nt(qm, k2) + bref[h]
                mx = jnp.max(sc, axis=1, keepdims=True)
                p = jnp.exp(sc - mx).astype(BF16)
                oe = _dot(p, jnp.concatenate([v2, ones], axis=1))
                outs.append(oe[:, 0:LANES])
                side = jnp.where(lane == h, mx, side)
                side = jnp.where(lane == N_SLOTS + h, oe[:, LANES:], side)
            o_ref[rr, r0:r0 + Q_BLK, l0:l0 + LANES] = jnp.where(lo, outs[0], outs[1]).astype(BF16)
        side_ref[rr, r0:r0 + Q_BLK, :] = side


ATTN_STEP_ROWS = 1024


def _attn_prompt(qkv, g):
    B, dil, L, _ = qkv.shape
    qb = min(ATTN_STEP_ROWS, L)
    nq = L // qb
    nsub = qb // Q_BLK
    rpb = min(dil, max(1, ATTN_STEP_ROWS // qb))
    b0, br = _prompt_bias_tables(dil)
    cur = lambda part: pl.BlockSpec((None, rpb, qb, ATT_W), lambda b, r, n: (b, r, n, part))
    prev = lambda part: pl.BlockSpec((None, rpb, Q_BLK, ATT_W),
                                     lambda b, r, n: (b, r, jnp.maximum(n * nsub - 1, 0), part))
    return pl.pallas_call(
        functools.partial(_attn_kernel, nsub=nsub, rpb=rpb),
        grid=(B, dil // rpb, nq),
        in_specs=[
            cur(0), prev(1), cur(1), prev(2), cur(2),
            pl.BlockSpec((None, N_SLOTS, Q_BLK, 2 * Q_BLK), lambda b, r, n: (jnp.minimum(n, 1), 0, 0, 0)),
            _const_spec((N_SLOTS, Q_BLK, 2 * Q_BLK), (0, 0, 0)),
        ],
        out_specs=[
            pl.BlockSpec((None, rpb, qb, ATT_W), lambda b, r, n: (b, r, n, 0)),
            pl.BlockSpec((None, rpb, qb, LANES), lambda b, r, n: (b, r, n, 0)),
        ],
        out_shape=[
            jax.ShapeDtypeStruct((B, dil, L, ATT_W), BF16),
            jax.ShapeDtypeStruct((B, dil, L, LANES), F32),
        ],
        compiler_params=pltpu.CompilerParams(
            dimension_semantics=("arbitrary", "arbitrary", "arbitrary"), vmem_limit_bytes=VMEM_LIMIT),
        name=f"attn_prompt_g{g}",
    )(qkv, qkv, qkv, qkv, qkv, jnp.asarray(b0), jnp.asarray(br))


def _merge_tail(x, gapa, sgb, zb, attn, wb_ref, wo_ref, g_ref, bt_ref):
    ab = (attn * zb.astype(F32)).astype(BF16)
    pb = _dot(ab, wb_ref[...])
    m = gapa.astype(F32) + sgb.astype(F32) * pb
    o = _dot(m.astype(BF16), wo_ref[...])
    return _layer_norm(ALPHA * x + o, g_ref[...], bt_ref[...])


def _merge_groups_kernel(x_ref, gapa_ref, sgb_ref, zb_ref, o1_ref, o2_ref, o3_ref, s1_ref, s2_ref, s3_ref,
                         ex_ref, wb_ref, wo_ref, g_ref, bt_ref, y_ref, nat_ref, nats_ref, *, tm):
    nsl = ATT_W // LANES
    for gi, (o_ref, s_ref) in enumerate(((o2_ref, s2_ref), (o3_ref, s3_ref))):
        dil = GROUPS[gi + 1][1]
        n = tm // dil
        for r in range(dil):
            blk = o_ref[r].astype(F32)
            for sl in range(nsl):
                nat_ref[gi, sl, pl.ds(r, n, stride=dil), :] = blk[:, sl * LANES:(sl + 1) * LANES]
            nats_ref[gi, pl.ds(r, n, stride=dil), :] = s_ref[r]
    sides = (s1_ref[...], nats_ref[0], nats_ref[1])
    mx = jnp.maximum(jnp.maximum(sides[0], sides[1]), sides[2])
    es = [jnp.exp(t - mx) for t in sides]
    dens = [pltpu.roll(t, LANES - N_SLOTS, axis=1) for t in sides]
    total = es[0] * dens[0] + es[1] * dens[1] + es[2] * dens[2]
    head_lane = lax.broadcasted_iota(jnp.int32, total.shape, 1) < N_SLOTS
    inv = jnp.where(head_lane, 1.0 / total, 0.0)
    ex = ex_ref[...]
    w1, w2, w3 = (_dot((e * inv).astype(BF16), ex) for e in es)
    slabs = []
    for sl in range(nsl):
        cs = slice(sl * LANES, (sl + 1) * LANES)
        slabs.append(w1[:, cs] * o1_ref[:, cs].astype(F32) + w2[:, cs] * nat_ref[0, sl] + w3[:, cs] * nat_ref[1, sl])
    attn = jnp.concatenate(slabs, axis=1)
    y_ref[...] = _merge_tail(x_ref[...], gapa_ref[...], sgb_ref[...], zb_ref[...], attn,
                             wb_ref, wo_ref, g_ref, bt_ref)


def _merge_plain_kernel(x_ref, gapa_ref, sgb_ref, zb_ref, attn_ref, wb_ref, wo_ref, g_ref, bt_ref, y_ref):
    y_ref[...] = _merge_tail(x_ref[...], gapa_ref[...], sgb_ref[...], zb_ref[...], attn_ref[...],
                             wb_ref, wo_ref, g_ref, bt_ref)


def _head_expand_matrix():
    ex = np.zeros((LANES, ATT_W), np.float32)
    for h in range(N_SLOTS):
        ex[h, h * HEAD_DIM:(h + 1) * HEAD_DIM] = 1.0
    return ex


def _merge_weight_specs():
    return [_const_spec((ATT_W, D_MODEL), (0, 0)), _const_spec((D_MODEL, D_MODEL), (0, 0)),
            _const_spec((1, D_MODEL), (0, 0)), _const_spec((1, D_MODEL), (0, 0))]


MERGE_ROWS = 1024


def _merge_groups(x, gapa, sgb, zb, parts, wb_bf, wo_bf, ln_g, ln_b, tm):
    B, S, _ = x.shape
    assert S % tm == 0 and tm % (GROUPS[-1][1] * 8) == 0
    tok = lambda width: pl.BlockSpec((None, tm, width), lambda b, s: (b, s, 0))
    grp = lambda dil, width: pl.BlockSpec((None, dil, tm // dil, width), lambda b, s: (b, 0, s, 0))
    (o1, s1), (o2, s2), (o3, s3) = parts
    d2, d3 = GROUPS[1][1], GROUPS[2][1]
    return pl.pallas_call(
        functools.partial(_merge_groups_kernel, tm=tm),
        grid=(B, S // tm),
        in_specs=[tok(D_MODEL), tok(D_MODEL), tok(D_MODEL), tok(ATT_W),
                  tok(ATT_W), grp(d2, ATT_W), grp(d3, ATT_W), tok(LANES), grp(d2, LANES), grp(d3, LANES),
                  _const_spec((LANES, ATT_W), (0, 0))] + _merge_weight_specs(),
        out_specs=tok(D_MODEL),
        out_shape=jax.ShapeDtypeStruct((B, S, D_MODEL), F32),
        scratch_shapes=[pltpu.VMEM((2, ATT_W // LANES, tm, LANES), F32), pltpu.VMEM((2, tm, LANES), F32)],
        compiler_params=pltpu.CompilerParams(
            dimension_semantics=("arbitrary", "arbitrary"), vmem_limit_bytes=VMEM_LIMIT),
        name="merge_groups",
    )(x, gapa, sgb, zb, o1.reshape(B, S, ATT_W), o2, o3, s1.reshape(B, S, LANES), s2, s3,
      jnp.asarray(_head_expand_matrix(), BF16), wb_bf, wo_bf, ln_g, ln_b)


def _merge_plain(x2, gapa, sgb, zb, attn, wb_bf, wo_bf, ln_g, ln_b):
    T = x2.shape[0]
    tok = lambda width: pl.BlockSpec((T, width), lambda t: (0, 0))
    return pl.pallas_call(
        _merge_plain_kernel,
        grid=(1,),
        in_specs=[tok(D_MODEL), tok(D_MODEL), tok(D_MODEL), tok(ATT_W), tok(ATT_W)] + _merge_weight_specs(),
        out_specs=tok(D_MODEL),
        out_shape=jax.ShapeDtypeStruct((T, D_MODEL), F32),
        compiler_params=pltpu.CompilerParams(dimension_semantics=("arbitrary",), vmem_limit_bytes=VMEM_LIMIT),
        name="merge_plain",
    )(x2, gapa, sgb, zb, attn, wb_bf, wo_bf, ln_g, ln_b)


def _sample_proj_kernel(x_ref, w_ref, b_ref, h_ref):
    h_ref[...] = _dot(x_ref[...].astype(BF16), w_ref[...]) + b_ref[...]


def _sample_proj(xs2, wbf, bsc):
    T = xs2.shape[0]
    cw = SAMPLE_PROJ_COLS
    return pl.pallas_call(
        _sample_proj_kernel,
        grid=(IN_W // cw,),
        in_specs=[
            pl.BlockSpec((T, D_MODEL), lambda j: (0, 0)),
            pl.BlockSpec((D_MODEL, cw), lambda j: (0, j)),
            pl.BlockSpec((1, cw), lambda j: (0, j)),
        ],
        out_specs=pl.BlockSpec((T, cw), lambda j: (0, j)),
        out_shape=jax.ShapeDtypeStruct((T, IN_W), F32),
        compiler_params=pltpu.CompilerParams(dimension_semantics=("arbitrary",), vmem_limit_bytes=VMEM_LIMIT),
        name="sample_proj",
    )(xs2, wbf, bsc)


def _sample_conv_kernel(h3_ref, hga_ref, hzb_ref, hgb_ref, st_ref, cw_ref, cb_ref, lg_ref, lb_ref, wa_ref,
                        gapa_ref, zb_ref, sgb_ref, cst_ref, hist_ref, slab_ref, cvt_ref, *, nb, tq):
    nh = CONV_WIDTH - 1
    a = h3_ref[:, O_AVAL:O_AVAL + CONV_CH]
    gt = h3_ref[:, O_AGATE:O_AGATE + CONV_CH]
    u = a * _sigmoid(gt)
    for sl in range(N_SLAB):
        slab_ref[sl] = u[:, sl * LANES:(sl + 1) * LANES]
    hist_ref[0:nh] = st_ref[...]
    for t in range(tq):
        hist_ref[nh + t] = jnp.concatenate(
            [slab_ref[sl, pl.ds(t, nb, stride=tq), :] for sl in range(N_SLAB)], axis=1)

    def conv_step(t, carry):
        acc = jnp.zeros((nb, CONV_CH), F32)
        for j in range(CONV_WIDTH):
            wj = jnp.concatenate([cw_ref[j * 8:(j + 1) * 8, :]] * (nb // 8), axis=0)
            acc = acc + hist_ref[t + j] * wj
        cvt_ref[t] = acc + cb_ref[...]
        return carry

    lax.fori_loop(0, tq, conv_step, 0)
    for t in range(tq):
        cvt = cvt_ref[t]
        for sl in range(N_SLAB):
            slab_ref[sl, pl.ds(t, nb, stride=tq), :] = cvt[:, sl * LANES:(sl + 1) * LANES]
    cst_ref[...] = hist_ref[tq:tq + nh]

    cv = jnp.concatenate([slab_ref[sl] for sl in range(N_SLAB)], axis=1)
    y = _layer_norm(cv, lg_ref[...], lb_ref[...])
    ca = (_silu(y) * _silu(h3_ref[:, O_ZA:O_ZA + CONV_CH])).astype(BF16)
    gapa_ref[...] = (_sigmoid(hga_ref[...]) * _dot(ca, wa_ref[...])).astype(BF16)
    zb_ref[...] = _silu(hzb_ref[...]).astype(BF16)
    sgb_ref[...] = _sigmoid(hgb_ref[...]).astype(BF16)


def _sample_conv_branch(hs, state_t, cw8, conv_b, lg, lb, wa_bf, nb, tq):
    T = nb * tq
    nh = CONV_WIDTH - 1
    col = lambda width, off: pl.BlockSpec((T, width), lambda i: (0, off // width))
    full = lambda shape: pl.BlockSpec(shape, lambda i: (0,) * len(shape))
    return pl.pallas_call(
        functools.partial(_sample_conv_kernel, nb=nb, tq=tq),
        grid=(1,),
        in_specs=[
            col(3 * CONV_CH, 0), col(D_MODEL, O_GA), col(ATT_W, O_ZB), col(D_MODEL, O_GB),
            full((nh, nb, CONV_CH)), full((CONV_WIDTH * 8, CONV_CH)),
            full((1, CONV_CH)), full((1, CONV_CH)), full((1, CONV_CH)), full((CONV_CH, D_MODEL)),
        ],
        out_specs=[full((T, D_MODEL)), full((T, ATT_W)), full((T, D_MODEL)), full((nh, nb, CONV_CH))],
        out_shape=[
            jax.ShapeDtypeStruct((T, D_MODEL), BF16),
            jax.ShapeDtypeStruct((T, ATT_W), BF16),
            jax.ShapeDtypeStruct((T, D_MODEL), BF16),
            jax.ShapeDtypeStruct((nh, nb, CONV_CH), F32),
        ],
        scratch_shapes=[pltpu.VMEM((nh + tq, nb, CONV_CH), F32), pltpu.VMEM((N_SLAB, T, LANES), F32),
                        pltpu.VMEM((tq, nb, CONV_CH), F32)],
        compiler_params=pltpu.CompilerParams(dimension_semantics=("arbitrary",), vmem_limit_bytes=VMEM_LIMIT),
        name="sample_conv_branch",
    )(hs, hs, hs, hs, state_t, cw8, conv_b, lg, lb, wa_bf)


def _sample_bias_tables(g, tq):
    win, dil = GROUPS[g]
    slopes = _alibi_slopes()
    tabc = np.full((N_SLOTS * tq, win), NEG, np.float32)
    tabn = np.full((N_SLOTS * tq, tq), NEG, np.float32)
    pos = np.arange(win)
    for h in range(N_SLOTS):
        for t in range(tq):
            row = h * tq + t
            dist = win + t - pos
            ok = (dist % dil == 0) & (dist // dil <= win // dil)
            tabc[row, ok] = -slopes[h] * dist[ok].astype(np.float32)
            for t2 in range(t + 1):
                if (t - t2) % dil == 0:
                    tabn[row, t2] = -slopes[h] * np.float32(t - t2)
    return tabc, tabn


def _sample_attn_step(kv, q_ref, k_ref, v_ref, c1_ref, c2_ref, c3_ref, tc1_ref, tc2_ref, tc3_ref,
                      tn1_ref, tn2_ref, tn3_ref, hm_ref, o_ref,
                      p1_ref, p2_ref, p3_ref, pn_ref, den_ref, *, tq):
    caches = (c1_ref, c2_ref, c3_ref)
    probs = (p1_ref, p2_ref, p3_ref)
    tabc = (tc1_ref, tc2_ref, tc3_ref)
    tabn = (tn1_ref, tn2_ref, tn3_ref)
    hm = hm_ref[...]

    @pl.when(kv == 0)
    def _():
        scores, mx = [], None
        for g in range(N_GROUPS):
            c0 = g * ATT_W
            qbd = (jnp.concatenate([q_ref[:, c0:c0 + ATT_W]] * N_SLOTS, axis=0) * hm).astype(BF16)
            sc = _dot(qbd, caches[g][...].astype(BF16)) + tabc[g][...]
            sn = _dot_nt(qbd, k_ref[:, c0:c0 + ATT_W].astype(BF16)) + tabn[g][...]
            m = jnp.maximum(jnp.max(sc, axis=1, keepdims=True), jnp.max(sn, axis=1, keepdims=True))
            mx = m if mx is None else jnp.maximum(mx, m)
            scores.append((sc, sn))
        den = jnp.zeros((N_SLOTS * tq, 1), F32)
        for g in range(N_GROUPS):
            sc, sn = scores[g]
            p = jnp.exp(sc - mx)
            pn = jnp.exp(sn - mx)
            den = den + jnp.sum(p, axis=1, keepdims=True) + jnp.sum(pn, axis=1, keepdims=True)
            probs[g][...] = p.astype(BF16)
            pn_ref[g] = pn
        den_ref[...] = den

    @pl.when(kv == 1)
    def _():
        acc = jnp.zeros((N_SLOTS * tq, ATT_W), F32)
        for g in range(N_GROUPS):
            c0 = g * ATT_W
            acc = acc + _dot_nt(probs[g][...], caches[g][...].astype(BF16))
            acc = acc + _dot(pn_ref[g].astype(BF16), v_ref[:, c0:c0 + ATT_W].astype(BF16))
        out = (acc / den_ref[...]) * hm
        o_ref[...] = jnp.sum(out.reshape(N_SLOTS, tq, ATT_W), axis=0)


def _sample_attn_operands(hs3, caches_t, nb, tq, step_of):
    seq = lambda *ids: step_of(*ids) // 2
    half = lambda *ids: step_of(*ids) % 2
    cspecs, tcs, tns = [], [], []
    for g, (win, _) in enumerate(GROUPS):
        cspecs.append(pl.BlockSpec((None, None, ATT_W, win), lambda *ids: (seq(*ids), half(*ids), 0, 0)))
        tc, tn = _sample_bias_tables(g, tq)
        tcs.append(jnp.asarray(tc))
        tns.append(jnp.asarray(tn))
    hm = np.zeros((N_SLOTS * tq, ATT_W), np.float32)
    for h in range(N_SLOTS):
        hm[h * tq:(h + 1) * tq, h * HEAD_DIM:(h + 1) * HEAD_DIM] = 1.0
    qkv = lambda off: pl.BlockSpec((None, tq, QKV_W), lambda *ids: (seq(*ids), 0, off // QKV_W))
    const = lambda a: _const_spec(a.shape, (0,) * a.ndim)
    nrow = N_SLOTS * tq
    in_specs = ([qkv(O_Q), qkv(O_K), qkv(O_V)] + cspecs + [const(t) for t in tcs] + [const(t) for t in tns]
                + [_const_spec(hm.shape, (0, 0))])
    args = [hs3, hs3, hs3, *caches_t, *tcs, *tns, jnp.asarray(hm)]
    out_spec = pl.BlockSpec((None, tq, ATT_W), lambda *ids: (seq(*ids), 0, 0))
    out_shape = jax.ShapeDtypeStruct((nb, tq, ATT_W), F32)
    scratch = ([pltpu.VMEM((nrow, win), BF16) for win, _ in GROUPS]
               + [pltpu.VMEM((N_GROUPS, nrow, tq), F32), pltpu.VMEM((nrow, 1), F32)])
    return in_specs, args, out_spec, out_shape, scratch


def _to_feature_major(c):
    n, w = c.shape[0], c.shape[1]
    return jnp.transpose(c, (0, 2, 3, 4, 1)).reshape(n, 2, ATT_W, w)


def _from_feature_major(c):
    n, _, _, w = c.shape
    return jnp.transpose(c.reshape(n, 2, N_SLOTS, HEAD_DIM, w), (0, 4, 1, 2, 3))


def kernel(x_prompt, x_sample, cache_kv_w128, cache_kv_w512, cache_kv_w2048, state_conv, w_in, b_in, conv_w, conv_b,
           conv_ln_g, conv_ln_b, w_a, w_b, w_out, ln_g, ln_b):
    B, S, _ = x_prompt.shape
    nb, tq, _ = x_sample.shape
    caches = (cache_kv_w128, cache_kv_w512, cache_kv_w2048)
    tm = PROMPT_ROWS
    for (win, dil), c in zip(GROUPS, caches):
        assert c.shape[1] == win and S % (dil * Q_BLK) == 0 and S >= win and tm % (dil * 16) == 0
    assert tq == 8 and nb % 8 == 0 and state_conv.shape[1] == CONV_WIDTH - 1 and S % tm == 0

    col_scale = np.ones((IN_W,), np.float32)
    col_scale[O_Q:O_K] = HEAD_DIM ** -0.5
    wbf = (w_in * col_scale).astype(BF16)
    b1 = b_in * col_scale
    bsc = b1.reshape(1, IN_W)
    wkvt = lax.optimization_barrier(w_in[:, O_K:O_ZB]).T.astype(BF16)
    wkt, wvt = wkvt[:QKV_W], wkvt[QKV_W:]
    bkt, bvt = b1[O_K:O_V].reshape(QKV_W, 1), b1[O_V:O_ZB].reshape(QKV_W, 1)
    cw8 = jnp.broadcast_to(conv_w[:, None, :], (CONV_WIDTH, 8, CONV_CH)).reshape(CONV_WIDTH * 8, CONV_CH)
    cb, clg, clb = (t.reshape(1, CONV_CH) for t in (conv_b, conv_ln_g, conv_ln_b))
    lg, lb = ln_g.reshape(1, D_MODEL), ln_b.reshape(1, D_MODEL)
    wa_bf, wb_bf, wo_bf = w_a.astype(BF16), w_b.astype(BF16), w_out.astype(BF16)

    Ts = nb * tq
    hs = _sample_proj(x_sample.reshape(Ts, D_MODEL), wbf, bsc)
    hs3 = hs.reshape(nb, tq, IN_W)
    kv_new = jnp.stack([hs3[:, :, O_K:O_V], hs3[:, :, O_V:O_ZB]], axis=1)
    caches_t = [_to_feature_major(c) for c in caches]

    gapa, cst_p, *kvs_rows = _conv_branch_prompt(x_prompt, wbf, bsc, cw8, cb, clg, clb, wa_bf, kv_new, caches_t, tm)
    qkv1, qkv2, qkv3, zb, sgb, attn_s = _qkv_prompt(x_prompt, wbf, bsc, hs3, caches_t, tm)
    kvt = _kv_window(x_prompt, wkt, wvt, bkt, bvt, tm)
    parts = [_attn_prompt(t, g) for g, t in enumerate((qkv1, qkv2, qkv3))]
    y_p = _merge_groups(x_prompt, gapa, sgb, zb, parts, wb_bf, wo_bf, lg, lb, MERGE_ROWS)
    kv_p = [_from_feature_major(t) for t in kvt]
    conv_p = cst_p[:, HOFF:, :]

    state_t = jnp.transpose(state_conv, (1, 0, 2))
    gapa_s, zb_s, sgb_s, cst_t = _sample_conv_branch(hs, state_t, cw8, cb, clg, clb, wa_bf, nb, tq)
    y_s = _merge_plain(x_sample.reshape(Ts, D_MODEL), gapa_s, sgb_s, zb_s, attn_s.reshape(Ts, ATT_W),
                       wb_bf, wo_bf, lg, lb).reshape(nb, tq, D_MODEL)
    kv_s = [_from_feature_major(t.reshape(nb, 2, ATT_W, t.shape[-1])) for t in kvs_rows]
    conv_s = jnp.transpose(cst_t, (1, 0, 2))
    return (y_p, y_s, kv_p[0], kv_p[1], kv_p[2], conv_p, kv_s[0], kv_s[1], kv_s[2], conv_s)
```

```python
import functools

import numpy as np
import jax
import jax.numpy as jnp
from jax import lax
from jax.experimental import pallas as pl
from jax.experimental.pallas import tpu as pltpu

F32 = jnp.float32
BF16 = jnp.bfloat16

D_MODEL = 1024
CONV_CH = 1024
CONV_WIDTH = 31
N_SLOTS = 8
HEAD_DIM = 64
GROUPS = ((128, 1), (512, 4), (2048, 16))
N_GROUPS = len(GROUPS)
ATT_W = N_SLOTS * HEAD_DIM
QKV_W = N_GROUPS * ATT_W
Q_BLK = 128
DEPTH = 1
ALPHA = (2.0 * DEPTH) ** 0.25
LN_EPS = 1e-5
NEG = -1e30

O_AVAL, O_AGATE, O_ZA = 0, CONV_CH, 2 * CONV_CH
O_Q = 3 * CONV_CH
O_K = O_Q + QKV_W
O_V = O_K + QKV_W
O_ZB = O_V + QKV_W
O_GA = O_ZB + ATT_W
O_GB = O_GA + D_MODEL
IN_W = O_GB + D_MODEL

HIST = 32
HOFF = HIST - (CONV_WIDTH - 1)
LANES = 128
VMEM_LIMIT = 56 * 1024 * 1024
PROMPT_ROWS = 512
PROJ_COLS = 256
CONV_ROWS = 64
GATE_COLS = 512
SAMPLE_PROJ_COLS = 2048
KV_WINDOW_ROWS = 1024


def _dot(a, b):
    return jnp.dot(a, b, preferred_element_type=F32)


def _dot_nt(a, b):
    return lax.dot_general(a, b, (((1,), (1,)), ((), ())), preferred_element_type=F32)


def _sigmoid(x):
    return 0.5 * jnp.tanh(0.5 * x) + 0.5


def _silu(x):
    h = 0.5 * x
    return h * jnp.tanh(h) + h


def _layer_norm(x, g, b):
    mu = jnp.mean(x, axis=-1, keepdims=True)
    xc = x - mu
    var = jnp.mean(xc * xc, axis=-1, keepdims=True)
    return xc * lax.rsqrt(var + LN_EPS) * g + b


def _const_spec(shape, index):
    return pl.BlockSpec(shape, lambda *_: index, pipeline_mode=pl.Buffered(1))


def _alibi_slopes():
    return (2.0 ** (-8.0 * np.arange(1, N_SLOTS + 1, dtype=np.float64) / N_SLOTS)).astype(np.float32)


SHIFT_ROWS = 64


def _stage_new_rows(new_ref, newt_ref, tq):
    for g in range(N_GROUPS):
        for i in range(new_ref.shape[0]):
            rows = new_ref[i, :, g * ATT_W:(g + 1) * ATT_W]
            padded = jnp.concatenate([jnp.zeros((LANES - tq, ATT_W), F32), rows], axis=0)
            newt_ref[g, i * ATT_W:(i + 1) * ATT_W, :] = padded.T


def _shift_cache_rows(c_ref, newt_ref, u_ref, g, r0, tq):
    win = GROUPS[g][0]
    lane = lax.broadcasted_iota(jnp.int32, (SHIFT_ROWS, LANES), 1)
    rows = pl.ds(r0, SHIFT_ROWS)
    rolled = pltpu.roll(c_ref[rows, :], win - tq, axis=1)
    if win > LANES:
        u_ref[rows, 0:win - LANES] = rolled[:, 0:win - LANES]
    u_ref[rows, win - LANES:] = jnp.where(lane >= LANES - tq, newt_ref[g, rows, :], rolled[:, win - LANES:])


def _conv_branch_kernel(x_ref, w3_ref, wga_ref, b3_ref, bga_ref, cw_ref, cb_ref, lg_ref, lb_ref, wa_ref,
                        new_ref, c1_ref, c2_ref, c3_ref,
                        gapa_ref, cst_ref, u1_ref, u2_ref, u3_ref, uh_ref, cv_ref, ca_ref, newt_ref,
                        *, tm, ns, tq):
    s = pl.program_id(1)
    ch = PROJ_COLS
    _stage_new_rows(new_ref, newt_ref, tq)

    @pl.when(s == 0)
    def _():
        uh_ref[0:HIST, :] = jnp.zeros((HIST, CONV_CH), F32)

    xb = x_ref[...].astype(BF16)

    def proj(c0, n):
        return _dot(xb, w3_ref[:, c0:c0 + n]) + b3_ref[:, c0:c0 + n]

    for c in range(0, CONV_CH, ch):
        a = proj(O_AVAL + c, ch)
        gt = proj(O_AGATE + c, ch)
        uh_ref[HIST:HIST + tm, c:c + ch] = a * _sigmoid(gt)

    rc = CONV_ROWS

    def conv_rows(i, carry):
        base = pl.multiple_of(HIST - 8 + i * rc, 8)
        for c in range(0, CONV_CH, ch):
            run = None
            for b in range(7, -1, -1):
                part = None
                for a in range((CONV_WIDTH - 1 - b) // 8 + 1):
                    j = CONV_WIDTH - 1 - (8 * a + b)
                    wj = jnp.concatenate([cw_ref[j * 8:(j + 1) * 8, c:c + ch]] * (rc // 8 + 1), axis=0)
                    term = uh_ref[pl.ds(base - 8 * a, rc + 8), c:c + ch] * wj
                    part = term if part is None else part + term
                run = part if run is None else part + pltpu.roll(run, 1, axis=0)
            cv_ref[pl.ds(pl.multiple_of(i * rc, rc), rc), c:c + ch] = run[8:, :] + cb_ref[:, c:c + ch]
        for g, (c_ref, u_ref) in enumerate(((c1_ref, u1_ref), (c2_ref, u2_ref), (c3_ref, u3_ref))):
            for pair in range(c_ref.shape[0] // ATT_W):
                r0 = pl.multiple_of(pair * ATT_W + i * SHIFT_ROWS, SHIFT_ROWS)
                _shift_cache_rows(c_ref, newt_ref, u_ref, g, r0, tq)
        return carry

    assert (tm // rc) * SHIFT_ROWS == ATT_W
    lax.fori_loop(0, tm // rc, conv_rows, 0)

    @pl.when(s == ns - 1)
    def _():
        cst_ref[...] = uh_ref[tm:tm + HIST, :]

    uh_ref[0:HIST, :] = uh_ref[tm:tm + HIST, :]

    x = cv_ref[...]
    mu = jnp.mean(x, axis=-1, keepdims=True)
    xc = x - mu
    var = jnp.mean(xc * xc, axis=-1, keepdims=True)
    rstd = lax.rsqrt(var + LN_EPS)
    for c in range(0, CONV_CH, ch):
        y = (cv_ref[:, c:c + ch] - mu) * rstd * lg_ref[:, c:c + ch] + lb_ref[:, c:c + ch]
        za = proj(O_ZA + c, ch)
        ca_ref[:, c:c + ch] = (_silu(y) * _silu(za)).astype(BF16)

    ca = ca_ref[...]
    for c in range(0, D_MODEL, ch):
        pa = _dot(ca, wa_ref[:, c:c + ch])
        ga = _dot(xb, wga_ref[:, c:c + ch]) + bga_ref[:, c:c + ch]
        gapa_ref[:, c:c + ch] = (_sigmoid(ga) * pa).astype(BF16)


def _conv_branch_prompt(x, wbf, bsc, cw8, conv_b, lg, lb, wa_bf, kv_new, caches_t, tm):
    B, S, _ = x.shape
    ns = S // tm
    nb, _, tq, _ = kv_new.shape
    pairs = 2 * nb
    assert pairs % (B * ns) == 0
    ppb = pairs // (B * ns)
    kern = functools.partial(_conv_branch_kernel, tm=tm, ns=ns, tq=tq)
    cache_specs = [pl.BlockSpec((ppb * ATT_W, win), lambda b, s: (b * ns + s, 0)) for win, _ in GROUPS]
    cache_rows = [c.reshape(pairs * ATT_W, c.shape[-1]) for c in caches_t]
    return pl.pallas_call(
        kern,
        grid=(B, ns),
        in_specs=[
            pl.BlockSpec((None, tm, D_MODEL), lambda b, s: (b, s, 0)),
            _const_spec((D_MODEL, 3 * CONV_CH), (0, 0)),
            _const_spec((D_MODEL, D_MODEL), (0, O_GA // D_MODEL)),
            _const_spec((1, 3 * CONV_CH), (0, 0)),
            _const_spec((1, D_MODEL), (0, O_GA // D_MODEL)),
            _const_spec((CONV_WIDTH * 8, CONV_CH), (0, 0)),
            _const_spec((1, CONV_CH), (0, 0)),
            _const_spec((1, CONV_CH), (0, 0)),
            _const_spec((1, CONV_CH), (0, 0)),
            _const_spec((CONV_CH, D_MODEL), (0, 0)),
            pl.BlockSpec((ppb, tq, QKV_W), lambda b, s: (b * ns + s, 0, 0)),
        ] + cache_specs,
        out_specs=[
            pl.BlockSpec((None, tm, D_MODEL), lambda b, s: (b, s, 0)),
            pl.BlockSpec((None, HIST, CONV_CH), lambda b, s: (b, 0, 0)),
        ] + cache_specs,
        out_shape=[
            jax.ShapeDtypeStruct((B, S, D_MODEL), BF16),
            jax.ShapeDtypeStruct((B, HIST, CONV_CH), F32),
        ] + [jax.ShapeDtypeStruct(c.shape, F32) for c in cache_rows],
        scratch_shapes=[
            pltpu.VMEM((HIST + tm, CONV_CH), F32),
            pltpu.VMEM((tm, CONV_CH), F32),
            pltpu.VMEM((tm, CONV_CH), BF16),
            pltpu.VMEM((N_GROUPS, ppb * ATT_W, LANES), F32),
        ],
        compiler_params=pltpu.CompilerParams(
            dimension_semantics=("arbitrary", "arbitrary"), vmem_limit_bytes=VMEM_LIMIT),
        name="conv_branch_prompt",
    )(x, wbf, wbf, bsc, bsc, cw8, conv_b, lg, lb, wa_bf, kv_new.reshape(pairs, tq, QKV_W), *cache_rows)


N_SLAB = D_MODEL // LANES


N_SAMPLE_IN = 13
N_SAMPLE_SCRATCH = 5


def _qkv_kernel(x_ref, wq_ref, wk_ref, wv_ref, wzb_ref, wgb_ref, bq_ref, bk_ref, bv_ref, bzb_ref, bgb_ref,
                *rest, tm, ns, tq):
    sample_in = rest[:N_SAMPLE_IN]
    o1_ref, o2_ref, o3_ref, zb_ref, sgb_ref, so_ref = rest[N_SAMPLE_IN:N_SAMPLE_IN + 6]
    xs_ref = rest[N_SAMPLE_IN + 6]
    sample_scratch = rest[N_SAMPLE_IN + 7:]
    step = pl.program_id(0) * ns + pl.program_id(1)
    _sample_attn_step(step % 2, *sample_in, so_ref, *sample_scratch, tq=tq)
    xb = x_ref[...].astype(BF16)
    for sl in range(N_SLAB):
        xs_ref[sl] = x_ref[:, sl * LANES:(sl + 1) * LANES]
    outs = (o1_ref, o2_ref, o3_ref)
    for g, (_, dil) in enumerate(GROUPS):
        n = tm // dil
        if dil == 1:
            lhs = xb
        else:
            lhs = jnp.concatenate(
                [jnp.concatenate([xs_ref[sl, pl.ds(r, n, stride=dil), :] for sl in range(N_SLAB)], axis=1)
                 for r in range(dil)], axis=0).astype(BF16)
        c0 = g * ATT_W
        for part, (w_ref, b_ref) in enumerate(((wq_ref, bq_ref), (wk_ref, bk_ref), (wv_ref, bv_ref))):
            res = _dot(lhs, w_ref[:, c0:c0 + ATT_W]) + b_ref[:, c0:c0 + ATT_W]
            outs[g][:, :, part * ATT_W:(part + 1) * ATT_W] = res.astype(BF16).reshape(dil, n, ATT_W)
    zb = _dot(xb, wzb_ref[...]) + bzb_ref[...]
    zb_ref[...] = _silu(zb).astype(BF16)
    for c in range(0, D_MODEL, GATE_COLS):
        gb = _dot(xb, wgb_ref[:, c:c + GATE_COLS]) + bgb_ref[:, c:c + GATE_COLS]
        sgb_ref[:, c:c + GATE_COLS] = _sigmoid(gb).astype(BF16)


def _qkv_prompt(x, wbf, bsc, hs3, caches_t, tm):
    B, S, _ = x.shape
    ns = S // tm
    nb, tq, _ = hs3.shape
    assert B * ns == 2 * nb
    s_in, s_args, s_out_spec, s_out_shape, s_scratch = _sample_attn_operands(
        hs3, caches_t, nb, tq, lambda b, s: b * ns + s)
    assert len(s_in) == N_SAMPLE_IN and len(s_scratch) == N_SAMPLE_SCRATCH
    tok = lambda width: pl.BlockSpec((None, tm, width), lambda b, s: (b, s, 0))
    grp = lambda dil: pl.BlockSpec((None, dil, tm // dil, QKV_W), lambda b, s: (b, 0, s, 0))
    wspec = lambda width, off: _const_spec((D_MODEL, width), (0, off // width))
    bspec = lambda width, off: _const_spec((1, width), (0, off // width))
    return pl.pallas_call(
        functools.partial(_qkv_kernel, tm=tm, ns=ns, tq=tq),
        grid=(B, ns),
        in_specs=[
            tok(D_MODEL),
            wspec(QKV_W, O_Q), wspec(QKV_W, O_K), wspec(QKV_W, O_V), wspec(ATT_W, O_ZB), wspec(D_MODEL, O_GB),
            bspec(QKV_W, O_Q), bspec(QKV_W, O_K), bspec(QKV_W, O_V), bspec(ATT_W, O_ZB), bspec(D_MODEL, O_GB),
        ] + s_in,
        out_specs=[grp(d) for _, d in GROUPS] + [tok(ATT_W), tok(D_MODEL), s_out_spec],
        out_shape=[jax.ShapeDtypeStruct((B, d, S // d, QKV_W), BF16) for _, d in GROUPS] + [
            jax.ShapeDtypeStruct((B, S, ATT_W), BF16),
            jax.ShapeDtypeStruct((B, S, D_MODEL), BF16),
            s_out_shape,
        ],
        scratch_shapes=[pltpu.VMEM((N_SLAB, tm, LANES), F32)] + s_scratch,
        compiler_params=pltpu.CompilerParams(
            dimension_semantics=("arbitrary", "arbitrary"), vmem_limit_bytes=VMEM_LIMIT),
        name="qkv_prompt",
    )(x, wbf, wbf, wbf, wbf, wbf, bsc, bsc, bsc, bsc, bsc, *s_args)


def _kv_window_kernel(x_ref, wkt_ref, wvt_ref, bkt_ref, bvt_ref, o1_ref, o2_ref, o3_ref, *, tw, nw, keeps):
    j = pl.program_id(1)
    xb = x_ref[...].astype(BF16)
    for g, o_ref in enumerate((o1_ref, o2_ref, o3_ref)):
        rows = min(keeps[g], tw)
        nwt = keeps[g] // rows
        c0 = g * ATT_W

        @pl.when(j >= nw - nwt)
        def _(o_ref=o_ref, rows=rows, c0=c0):
            xw = xb[tw - rows:, :]
            o_ref[0] = _dot_nt(wkt_ref[c0:c0 + ATT_W, :], xw) + bkt_ref[c0:c0 + ATT_W, :]
            o_ref[1] = _dot_nt(wvt_ref[c0:c0 + ATT_W, :], xw) + bvt_ref[c0:c0 + ATT_W, :]


def _kv_window(x, wkt, wvt, bkt, bvt, tw):
    B, S, _ = x.shape
    keeps = tuple(min(w, S) for w, _ in GROUPS)
    nw = max(keeps) // tw
    first = S // tw - nw

    def out_spec(keep):
        rows = min(keep, tw)
        f = nw - keep // rows
        return pl.BlockSpec((None, 2, ATT_W, rows), lambda b, j: (b, 0, 0, jnp.maximum(j - f, 0)))

    return pl.pallas_call(
        functools.partial(_kv_window_kernel, tw=tw, nw=nw, keeps=keeps),
        grid=(B, nw),
        in_specs=[
            pl.BlockSpec((None, tw, D_MODEL), lambda b, j: (b, first + j, 0)),
            _const_spec((QKV_W, D_MODEL), (0, 0)), _const_spec((QKV_W, D_MODEL), (0, 0)),
            _const_spec((QKV_W, 1), (0, 0)), _const_spec((QKV_W, 1), (0, 0)),
        ],
        out_specs=[out_spec(k) for k in keeps],
        out_shape=[jax.ShapeDtypeStruct((B, 2, ATT_W, k), F32) for k in keeps],
        compiler_params=pltpu.CompilerParams(
            dimension_semantics=("arbitrary", "arbitrary"), vmem_limit_bytes=VMEM_LIMIT),
        name="kv_window",
    )(x, wkt, wvt, bkt, bvt)


def _prompt_bias_tables(dil):
    slopes = _alibi_slopes()
    i = np.arange(Q_BLK)[:, None]
    j = np.arange(2 * Q_BLK)[None, :]
    delta = i + Q_BLK - j
    band = (delta >= 0) & (delta <= Q_BLK)
    bias = -slopes[:, None, None] * (delta * dil).astype(np.float32)[None]
    rest = np.where(band[None], bias, np.float32(NEG)).astype(np.float32)
    first = np.where((band & (j >= Q_BLK))[None], bias, np.float32(NEG)).astype(np.float32)
    return np.stack([first, rest]), rest


def _attn_kernel(q_ref, kp_ref, kc_ref, vp_ref, vc_ref, b0_ref, br_ref, o_ref, side_ref, *, nsub, rpb):
    lane = lax.broadcasted_iota(jnp.int32, (Q_BLK, LANES), 1)
    lo = lane < HEAD_DIM
    m_lo = lo.astype(F32).astype(BF16)
    m_hi = (1.0 - lo.astype(F32)).astype(BF16)
    ones = jnp.ones((2 * Q_BLK, LANES), BF16)
    for rr, i in ((rr, i) for rr in range(rpb) for i in range(nsub)):
        r0 = i * Q_BLK
        side = jnp.zeros((Q_BLK, LANES), F32)
        for hp in range(N_SLOTS // 2):
            l0 = hp * LANES
            q2 = q_ref[rr, r0:r0 + Q_BLK, l0:l0 + LANES]
            if i == 0:
                k2 = jnp.concatenate([kp_ref[rr, :, l0:l0 + LANES], kc_ref[rr, 0:Q_BLK, l0:l0 + LANES]], axis=0)
                v2 = jnp.concatenate([vp_ref[rr, :, l0:l0 + LANES], vc_ref[rr, 0:Q_BLK, l0:l0 + LANES]], axis=0)
                bref = b0_ref
            else:
                k2 = kc_ref[rr, r0 - Q_BLK:r0 + Q_BLK, l0:l0 + LANES]
                v2 = vc_ref[rr, r0 - Q_BLK:r0 + Q_BLK, l0:l0 + LANES]
                bref = br_ref
            outs = []
            for par in range(2):
                h = 2 * hp + par
                qm = q2 * (m_lo if par == 0 else m_hi)
                sc = _dot_nt(qm, k2) + bref[h]
                mx = jnp.max(sc, axis=1, keepdims=True)
                p = jnp.exp(sc - mx).astype(BF16)
                oe = _dot(p, jnp.concatenate([v2, ones], axis=1))
                outs.append(oe[:, 0:LANES])
                side = jnp.where(lane == h, mx, side)
                side = jnp.where(lane == N_SLOTS + h, oe[:, LANES:], side)
            o_ref[rr, r0:r0 + Q_BLK, l0:l0 + LANES] = jnp.where(lo, outs[0], outs[1]).astype(BF16)
        side_ref[rr, r0:r0 + Q_BLK, :] = side


ATTN_STEP_ROWS = 2048


def _attn_prompt(qkv, g):
    B, dil, L, _ = qkv.shape
    qb = min(ATTN_STEP_ROWS, L)
    nq = L // qb
    nsub = qb // Q_BLK
    rpb = min(dil, max(1, ATTN_STEP_ROWS // qb))
    b0, br = _prompt_bias_tables(dil)
    cur = lambda part: pl.BlockSpec((None, rpb, qb, ATT_W), lambda b, r, n: (b, r, n, part))
    prev = lambda part: pl.BlockSpec((None, rpb, Q_BLK, ATT_W),
                                     lambda b, r, n: (b, r, jnp.maximum(n * nsub - 1, 0), part))
    return pl.pallas_call(
        functools.partial(_attn_kernel, nsub=nsub, rpb=rpb),
        grid=(B, dil // rpb, nq),
        in_specs=[
            cur(0), prev(1), cur(1), prev(2), cur(2),
            pl.BlockSpec((None, N_SLOTS, Q_BLK, 2 * Q_BLK), lambda b, r, n: (jnp.minimum(n, 1), 0, 0, 0)),
            _const_spec((N_SLOTS, Q_BLK, 2 * Q_BLK), (0, 0, 0)),
        ],
        out_specs=[
            pl.BlockSpec((None, rpb, qb, ATT_W), lambda b, r, n: (b, r, n, 0)),
            pl.BlockSpec((None, rpb, qb, LANES), lambda b, r, n: (b, r, n, 0)),
        ],
        out_shape=[
            jax.ShapeDtypeStruct((B, dil, L, ATT_W), BF16),
            jax.ShapeDtypeStruct((B, dil, L, LANES), F32),
        ],
        compiler_params=pltpu.CompilerParams(
            dimension_semantics=("arbitrary", "arbitrary", "arbitrary"), vmem_limit_bytes=VMEM_LIMIT),
        name=f"attn_prompt_g{g}",
    )(qkv, qkv, qkv, qkv, qkv, jnp.asarray(b0), jnp.asarray(br))


def _merge_tail(x, gapa, sgb, zb, attn, wb_ref, wo_ref, g_ref, bt_ref):
    ab = (attn * zb.astype(F32)).astype(BF16)
    pb = _dot(ab, wb_ref[...])
    m = gapa.astype(F32) + sgb.astype(F32) * pb
    o = _dot(m.astype(BF16), wo_ref[...])
    return _layer_norm(ALPHA * x + o, g_ref[...], bt_ref[...])


def _merge_groups_kernel(x_ref, gapa_ref, sgb_ref, zb_ref, o1_ref, o2_ref, o3_ref, s1_ref, s2_ref, s3_ref,
                         ex_ref, wb_ref, wo_ref, g_ref, bt_ref, y_ref, nat_ref, nats_ref, *, tm):
    nsl = ATT_W // LANES
    for gi, (o_ref, s_ref) in enumerate(((o2_ref, s2_ref), (o3_ref, s3_ref))):
        dil = GROUPS[gi + 1][1]
        n = tm // dil
        for r in range(dil):
            blk = o_ref[r].astype(F32)
            for sl in range(nsl):
                nat_ref[gi, sl, pl.ds(r, n, stride=dil), :] = blk[:, sl * LANES:(sl + 1) * LANES]
            nats_ref[gi, pl.ds(r, n, stride=dil), :] = s_ref[r]
    sides = (s1_ref[...], nats_ref[0], nats_ref[1])
    mx = jnp.maximum(jnp.maximum(sides[0], sides[1]), sides[2])
    es = [jnp.exp(t - mx) for t in sides]
    dens = [pltpu.roll(t, LANES - N_SLOTS, axis=1) for t in sides]
    total = es[0] * dens[0] + es[1] * dens[1] + es[2] * dens[2]
    head_lane = lax.broadcasted_iota(jnp.int32, total.shape, 1) < N_SLOTS
    inv = jnp.where(head_lane, 1.0 / total, 0.0)
    ex = ex_ref[...]
    w1, w2, w3 = (_dot((e * inv).astype(BF16), ex) for e in es)
    slabs = []
    for sl in range(nsl):
        cs = slice(sl * LANES, (sl + 1) * LANES)
        slabs.append(w1[:, cs] * o1_ref[:, cs].astype(F32) + w2[:, cs] * nat_ref[0, sl] + w3[:, cs] * nat_ref[1, sl])
    attn = jnp.concatenate(slabs, axis=1)
    y_ref[...] = _merge_tail(x_ref[...], gapa_ref[...], sgb_ref[...], zb_ref[...], attn,
                             wb_ref, wo_ref, g_ref, bt_ref)


def _merge_plain_kernel(x_ref, gapa_ref, sgb_ref, zb_ref, attn_ref, wb_ref, wo_ref, g_ref, bt_ref, y_ref):
    y_ref[...] = _merge_tail(x_ref[...], gapa_ref[...], sgb_ref[...], zb_ref[...], attn_ref[...],
                             wb_ref, wo_ref, g_ref, bt_ref)


def _head_expand_matrix():
    ex = np.zeros((LANES, ATT_W), np.float32)
    for h in range(N_SLOTS):
        ex[h, h * HEAD_DIM:(h + 1) * HEAD_DIM] = 1.0
    return ex


def _merge_weight_specs():
    return [_const_spec((ATT_W, D_MODEL), (0, 0)), _const_spec((D_MODEL, D_MODEL), (0, 0)),
            _const_spec((1, D_MODEL), (0, 0)), _const_spec((1, D_MODEL), (0, 0))]


MERGE_ROWS = 1024


def _merge_groups(x, gapa, sgb, zb, parts, wb_bf, wo_bf, ln_g, ln_b, tm):
    B, S, _ = x.shape
    assert S % tm == 0 and tm % (GROUPS[-1][1] * 8) == 0
    tok = lambda width: pl.BlockSpec((None, tm, width), lambda b, s: (b, s, 0))
    grp = lambda dil, width: pl.BlockSpec((None, dil, tm // dil, width), lambda b, s: (b, 0, s, 0))
    (o1, s1), (o2, s2), (o3, s3) = parts
    d2, d3 = GROUPS[1][1], GROUPS[2][1]
    return pl.pallas_call(
        functools.partial(_merge_groups_kernel, tm=tm),
        grid=(B, S // tm),
        in_specs=[tok(D_MODEL), tok(D_MODEL), tok(D_MODEL), tok(ATT_W),
                  tok(ATT_W), grp(d2, ATT_W), grp(d3, ATT_W), tok(LANES), grp(d2, LANES), grp(d3, LANES),
                  _const_spec((LANES, ATT_W), (0, 0))] + _merge_weight_specs(),
        out_specs=tok(D_MODEL),
        out_shape=jax.ShapeDtypeStruct((B, S, D_MODEL), F32),
        scratch_shapes=[pltpu.VMEM((2, ATT_W // LANES, tm, LANES), F32), pltpu.VMEM((2, tm, LANES), F32)],
        compiler_params=pltpu.CompilerParams(
            dimension_semantics=("arbitrary", "arbitrary"), vmem_limit_bytes=VMEM_LIMIT),
        name="merge_groups",
    )(x, gapa, sgb, zb, o1.reshape(B, S, ATT_W), o2, o3, s1.reshape(B, S, LANES), s2, s3,
      jnp.asarray(_head_expand_matrix(), BF16), wb_bf, wo_bf, ln_g, ln_b)


def _merge_plain(x2, gapa, sgb, zb, attn, wb_bf, wo_bf, ln_g, ln_b):
    T = x2.shape[0]
    tok = lambda width: pl.BlockSpec((T, width), lambda t: (0, 0))
    return pl.pallas_call(
        _merge_plain_kernel,
        grid=(1,),
        in_specs=[tok(D_MODEL), tok(D_MODEL), tok(D_MODEL), tok(ATT_W), tok(ATT_W)] + _merge_weight_specs(),
        out_specs=tok(D_MODEL),
        out_shape=jax.ShapeDtypeStruct((T, D_MODEL), F32),
        compiler_params=pltpu.CompilerParams(dimension_semantics=("arbitrary",), vmem_limit_bytes=VMEM_LIMIT),
        name="merge_plain",
    )(x2, gapa, sgb, zb, attn, wb_bf, wo_bf, ln_g, ln_b)


def _sample_proj_kernel(x_ref, w_ref, b_ref, h_ref):
    h_ref[...] = _dot(x_ref[...].astype(BF16), w_ref[...]) + b_ref[...]


def _sample_proj(xs2, wbf, bsc):
    T = xs2.shape[0]
    cw = SAMPLE_PROJ_COLS
    return pl.pallas_call(
        _sample_proj_kernel,
        grid=(IN_W // cw,),
        in_specs=[
            pl.BlockSpec((T, D_MODEL), lambda j: (0, 0)),
            pl.BlockSpec((D_MODEL, cw), lambda j: (0, j)),
            pl.BlockSpec((1, cw), lambda j: (0, j)),
        ],
        out_specs=pl.BlockSpec((T, cw), lambda j: (0, j)),
        out_shape=jax.ShapeDtypeStruct((T, IN_W), F32),
        compiler_params=pltpu.CompilerParams(dimension_semantics=("arbitrary",), vmem_limit_bytes=VMEM_LIMIT),
        name="sample_proj",
    )(xs2, wbf, bsc)


def _sample_conv_kernel(h3_ref, hga_ref, hzb_ref, hgb_ref, st_ref, cw_ref, cb_ref, lg_ref, lb_ref, wa_ref,
                        gapa_ref, zb_ref, sgb_ref, cst_ref, hist_ref, slab_ref, cvt_ref, *, nb, tq):
    nh = CONV_WIDTH - 1
    a = h3_ref[:, O_AVAL:O_AVAL + CONV_CH]
    gt = h3_ref[:, O_AGATE:O_AGATE + CONV_CH]
    u = a * _sigmoid(gt)
    for sl in range(N_SLAB):
        slab_ref[sl] = u[:, sl * LANES:(sl + 1) * LANES]
    hist_ref[0:nh] = st_ref[...]
    for t in range(tq):
        hist_ref[nh + t] = jnp.concatenate(
            [slab_ref[sl, pl.ds(t, nb, stride=tq), :] for sl in range(N_SLAB)], axis=1)

    def conv_step(t, carry):
        acc = jnp.zeros((nb, CONV_CH), F32)
        for j in range(CONV_WIDTH):
            wj = jnp.concatenate([cw_ref[j * 8:(j + 1) * 8, :]] * (nb // 8), axis=0)
            acc = acc + hist_ref[t + j] * wj
        cvt_ref[t] = acc + cb_ref[...]
        return carry

    lax.fori_loop(0, tq, conv_step, 0)
    for t in range(tq):
        cvt = cvt_ref[t]
        for sl in range(N_SLAB):
            slab_ref[sl, pl.ds(t, nb, stride=tq), :] = cvt[:, sl * LANES:(sl + 1) * LANES]
    cst_ref[...] = hist_ref[tq:tq + nh]

    cv = jnp.concatenate([slab_ref[sl] for sl in range(N_SLAB)], axis=1)
    y = _layer_norm(cv, lg_ref[...], lb_ref[...])
    ca = (_silu(y) * _silu(h3_ref[:, O_ZA:O_ZA + CONV_CH])).astype(BF16)
    gapa_ref[...] = (_sigmoid(hga_ref[...]) * _dot(ca, wa_ref[...])).astype(BF16)
    zb_ref[...] = _silu(hzb_ref[...]).astype(BF16)
    sgb_ref[...] = _sigmoid(hgb_ref[...]).astype(BF16)


def _sample_conv_branch(hs, state_t, cw8, conv_b, lg, lb, wa_bf, nb, tq):
    T = nb * tq
    nh = CONV_WIDTH - 1
    col = lambda width, off: pl.BlockSpec((T, width), lambda i: (0, off // width))
    full = lambda shape: pl.BlockSpec(shape, lambda i: (0,) * len(shape))
    return pl.pallas_call(
        functools.partial(_sample_conv_kernel, nb=nb, tq=tq),
        grid=(1,),
        in_specs=[
            col(3 * CONV_CH, 0), col(D_MODEL, O_GA), col(ATT_W, O_ZB), col(D_MODEL, O_GB),
            full((nh, nb, CONV_CH)), full((CONV_WIDTH * 8, CONV_CH)),
            full((1, CONV_CH)), full((1, CONV_CH)), full((1, CONV_CH)), full((CONV_CH, D_MODEL)),
        ],
        out_specs=[full((T, D_MODEL)), full((T, ATT_W)), full((T, D_MODEL)), full((nh, nb, CONV_CH))],
        out_shape=[
            jax.ShapeDtypeStruct((T, D_MODEL), BF16),
            jax.ShapeDtypeStruct((T, ATT_W), BF16),
            jax.ShapeDtypeStruct((T, D_MODEL), BF16),
            jax.ShapeDtypeStruct((nh, nb, CONV_CH), F32),
        ],
        scratch_shapes=[pltpu.VMEM((nh + tq, nb, CONV_CH), F32), pltpu.VMEM((N_SLAB, T, LANES), F32),
                        pltpu.VMEM((tq, nb, CONV_CH), F32)],
        compiler_params=pltpu.CompilerParams(dimension_semantics=("arbitrary",), vmem_limit_bytes=VMEM_LIMIT),
        name="sample_conv_branch",
    )(hs, hs, hs, hs, state_t, cw8, conv_b, lg, lb, wa_bf)


def _sample_bias_tables(g, tq):
    win, dil = GROUPS[g]
    slopes = _alibi_slopes()
    tabc = np.full((N_SLOTS * tq, win), NEG, np.float32)
    tabn = np.full((N_SLOTS * tq, tq), NEG, np.float32)
    pos = np.arange(win)
    for h in range(N_SLOTS):
        for t in range(tq):
            row = h * tq + t
            dist = win + t - pos
            ok = (dist % dil == 0) & (dist // dil <= win // dil)
            tabc[row, ok] = -slopes[h] * dist[ok].astype(np.float32)
            for t2 in range(t + 1):
                if (t - t2) % dil == 0:
                    tabn[row, t2] = -slopes[h] * np.float32(t - t2)
    return tabc, tabn


def _sample_attn_step(kv, q_ref, k_ref, v_ref, c1_ref, c2_ref, c3_ref, tc1_ref, tc2_ref, tc3_ref,
                      tn1_ref, tn2_ref, tn3_ref, hm_ref, o_ref,
                      p1_ref, p2_ref, p3_ref, pn_ref, den_ref, *, tq):
    caches = (c1_ref, c2_ref, c3_ref)
    probs = (p1_ref, p2_ref, p3_ref)
    tabc = (tc1_ref, tc2_ref, tc3_ref)
    tabn = (tn1_ref, tn2_ref, tn3_ref)
    hm = hm_ref[...]

    @pl.when(kv == 0)
    def _():
        scores, mx = [], None
        for g in range(N_GROUPS):
            c0 = g * ATT_W
            qbd = (jnp.concatenate([q_ref[:, c0:c0 + ATT_W]] * N_SLOTS, axis=0) * hm).astype(BF16)
            sc = _dot(qbd, caches[g][...].astype(BF16)) + tabc[g][...]
            sn = _dot_nt(qbd, k_ref[:, c0:c0 + ATT_W].astype(BF16)) + tabn[g][...]
            m = jnp.maximum(jnp.max(sc, axis=1, keepdims=True), jnp.max(sn, axis=1, keepdims=True))
            mx = m if mx is None else jnp.maximum(mx, m)
            scores.append((sc, sn))
        den = jnp.zeros((N_SLOTS * tq, 1), F32)
        for g in range(N_GROUPS):
            sc, sn = scores[g]
            p = jnp.exp(sc - mx)
            pn = jnp.exp(sn - mx)
            den = den + jnp.sum(p, axis=1, keepdims=True) + jnp.sum(pn, axis=1, keepdims=True)
            probs[g][...] = p.astype(BF16)
            pn_ref[g] = pn
        den_ref[...] = den

    @pl.when(kv == 1)
    def _():
        acc = jnp.zeros((N_SLOTS * tq, ATT_W), F32)
        for g in range(N_GROUPS):
            c0 = g * ATT_W
            acc = acc + _dot_nt(probs[g][...], caches[g][...].astype(BF16))
            acc = acc + _dot(pn_ref[g].astype(BF16), v_ref[:, c0:c0 + ATT_W].astype(BF16))
        out = (acc / den_ref[...]) * hm
        o_ref[...] = jnp.sum(out.reshape(N_SLOTS, tq, ATT_W), axis=0)


def _sample_attn_operands(hs3, caches_t, nb, tq, step_of):
    seq = lambda *ids: step_of(*ids) // 2
    half = lambda *ids: step_of(*ids) % 2
    cspecs, tcs, tns = [], [], []
    for g, (win, _) in enumerate(GROUPS):
        cspecs.append(pl.BlockSpec((None, None, ATT_W, win), lambda *ids: (seq(*ids), half(*ids), 0, 0)))
        tc, tn = _sample_bias_tables(g, tq)
        tcs.append(jnp.asarray(tc))
        tns.append(jnp.asarray(tn))
    hm = np.zeros((N_SLOTS * tq, ATT_W), np.float32)
    for h in range(N_SLOTS):
        hm[h * tq:(h + 1) * tq, h * HEAD_DIM:(h + 1) * HEAD_DIM] = 1.0
    qkv = lambda off: pl.BlockSpec((None, tq, QKV_W), lambda *ids: (seq(*ids), 0, off // QKV_W))
    const = lambda a: _const_spec(a.shape, (0,) * a.ndim)
    nrow = N_SLOTS * tq
    in_specs = ([qkv(O_Q), qkv(O_K), qkv(O_V)] + cspecs + [const(t) for t in tcs] + [const(t) for t in tns]
                + [_const_spec(hm.shape, (0, 0))])
    args = [hs3, hs3, hs3, *caches_t, *tcs, *tns, jnp.asarray(hm)]
    out_spec = pl.BlockSpec((None, tq, ATT_W), lambda *ids: (seq(*ids), 0, 0))
    out_shape = jax.ShapeDtypeStruct((nb, tq, ATT_W), F32)
    scratch = ([pltpu.VMEM((nrow, win), BF16) for win, _ in GROUPS]
               + [pltpu.VMEM((N_GROUPS, nrow, tq), F32), pltpu.VMEM((nrow, 1), F32)])
    return in_specs, args, out_spec, out_shape, scratch


def _to_feature_major(c):
    n, w = c.shape[0], c.shape[1]
    return jnp.transpose(c, (0, 2, 3, 4, 1)).reshape(n, 2, ATT_W, w)


def _from_feature_major(c):
    n, _, _, w = c.shape
    return jnp.transpose(c.reshape(n, 2, N_SLOTS, HEAD_DIM, w), (0, 4, 1, 2, 3))


def kernel(x_prompt, x_sample, cache_kv_w128, cache_kv_w512, cache_kv_w2048, state_conv, w_in, b_in, conv_w, conv_b,
           conv_ln_g, conv_ln_b, w_a, w_b, w_out, ln_g, ln_b):
    B, S, _ = x_prompt.shape
    nb, tq, _ = x_sample.shape
    caches = (cache_kv_w128, cache_kv_w512, cache_kv_w2048)
    tm = PROMPT_ROWS
    for (win, dil), c in zip(GROUPS, caches):
        assert c.shape[1] == win and S % (dil * Q_BLK) == 0 and S >= win and tm % (dil * 16) == 0
    assert tq == 8 and nb % 8 == 0 and state_conv.shape[1] == CONV_WIDTH - 1 and S % tm == 0

    col_scale = np.ones((IN_W,), np.float32)
    col_scale[O_Q:O_K] = HEAD_DIM ** -0.5
    wbf = (w_in * col_scale).astype(BF16)
    b1 = b_in * col_scale
    bsc = b1.reshape(1, IN_W)
    wkvt = lax.optimization_barrier(w_in[:, O_K:O_ZB]).T.astype(BF16)
    wkt, wvt = wkvt[:QKV_W], wkvt[QKV_W:]
    bkt, bvt = b1[O_K:O_V].reshape(QKV_W, 1), b1[O_V:O_ZB].reshape(QKV_W, 1)
    cw8 = jnp.broadcast_to(conv_w[:, None, :], (CONV_WIDTH, 8, CONV_CH)).reshape(CONV_WIDTH * 8, CONV_CH)
    cb, clg, clb = (t.reshape(1, CONV_CH) for t in (conv_b, conv_ln_g, conv_ln_b))
    lg, lb = ln_g.reshape(1, D_MODEL), ln_b.reshape(1, D_MODEL)
    wa_bf, wb_bf, wo_bf = w_a.astype(BF16), w_b.astype(BF16), w_out.astype(BF16)

    Ts = nb * tq
    hs = _sample_proj(x_sample.reshape(Ts, D_MODEL), wbf, bsc)
    hs3 = hs.reshape(nb, tq, IN_W)
    kv_new = jnp.stack([hs3[:, :, O_K:O_V], hs3[:, :, O_V:O_ZB]], axis=1)
    caches_t = [_to_feature_major(c) for c in caches]

    gapa, cst_p, *kvs_rows = _conv_branch_prompt(x_prompt, wbf, bsc, cw8, cb, clg, clb, wa_bf, kv_new, caches_t, tm)
    qkv1, qkv2, qkv3, zb, sgb, attn_s = _qkv_prompt(x_prompt, wbf, bsc, hs3, caches_t, tm)
    kvt = _kv_window(x_prompt, wkt, wvt, bkt, bvt, KV_WINDOW_ROWS)
    parts = [_attn_prompt(t, g) for g, t in enumerate((qkv1, qkv2, qkv3))]
    y_p = _merge_groups(x_prompt, gapa, sgb, zb, parts, wb_bf, wo_bf, lg, lb, MERGE_ROWS)
    kv_p = [_from_feature_major(t) for t in kvt]
    conv_p = cst_p[:, HOFF:, :]

    state_t = jnp.transpose(state_conv, (1, 0, 2))
    gapa_s, zb_s, sgb_s, cst_t = _sample_conv_branch(hs, state_t, cw8, cb, clg, clb, wa_bf, nb, tq)
    y_s = _merge_plain(x_sample.reshape(Ts, D_MODEL), gapa_s, sgb_s, zb_s, attn_s.reshape(Ts, ATT_W),
                       wb_bf, wo_bf, lg, lb).reshape(nb, tq, D_MODEL)
    kv_s = [_from_feature_major(t.reshape(nb, 2, ATT_W, t.shape[-1])) for t in kvs_rows]
    conv_s = jnp.transpose(cst_t, (1, 0, 2))
    return (y_p, y_s, kv_p[0], kv_p[1], kv_p[2], conv_p, kv_s[0], kv_s[1], kv_s[2], conv_s)
```

```python
import functools

import numpy as np
import jax
import jax.numpy as jnp
from jax import lax
from jax.experimental import pallas as pl
from jax.experimental.pallas import tpu as pltpu

F32 = jnp.float32
BF16 = jnp.bfloat16

D_MODEL = 1024
CONV_CH = 1024
CONV_WIDTH = 31
N_SLOTS = 8
HEAD_DIM = 64
GROUPS = ((128, 1), (512, 4), (2048, 16))
N_GROUPS = len(GROUPS)
ATT_W = N_SLOTS * HEAD_DIM
QKV_W = N_GROUPS * ATT_W
Q_BLK = 128
DEPTH = 1
ALPHA = (2.0 * DEPTH) ** 0.25
LN_EPS = 1e-5
NEG = -1e30

O_AVAL, O_AGATE, O_ZA = 0, CONV_CH, 2 * CONV_CH
O_Q = 3 * CONV_CH
O_K = O_Q + QKV_W
O_V = O_K + QKV_W
O_ZB = O_V + QKV_W
O_GA = O_ZB + ATT_W
O_GB = O_GA + D_MODEL
IN_W = O_GB + D_MODEL

HIST = 32
HOFF = HIST - (CONV_WIDTH - 1)
LANES = 128
VMEM_LIMIT = 56 * 1024 * 1024
PROMPT_ROWS = 512
PROJ_COLS = 256
CONV_ROWS = 64
GATE_COLS = 512
SAMPLE_PROJ_COLS = 2048
KV_WINDOW_ROWS = 1024


def _dot(a, b):
    return jnp.dot(a, b, preferred_element_type=F32)


def _dot_nt(a, b):
    return lax.dot_general(a, b, (((1,), (1,)), ((), ())), preferred_element_type=F32)


def _sigmoid(x):
    return 0.5 * jnp.tanh(0.5 * x) + 0.5


def _silu(x):
    h = 0.5 * x
    return h * jnp.tanh(h) + h


def _layer_norm(x, g, b):
    mu = jnp.mean(x, axis=-1, keepdims=True)
    xc = x - mu
    var = jnp.mean(xc * xc, axis=-1, keepdims=True)
    return xc * lax.rsqrt(var + LN_EPS) * g + b


def _const_spec(shape, index):
    return pl.BlockSpec(shape, lambda *_: index, pipeline_mode=pl.Buffered(1))


def _alibi_slopes():
    return (2.0 ** (-8.0 * np.arange(1, N_SLOTS + 1, dtype=np.float64) / N_SLOTS)).astype(np.float32)


SHIFT_ROWS = 64


def _stage_new_rows(new_ref, newt_ref, tq):
    for g in range(N_GROUPS):
        for i in range(new_ref.shape[0]):
            rows = new_ref[i, :, g * ATT_W:(g + 1) * ATT_W]
            padded = jnp.concatenate([jnp.zeros((LANES - tq, ATT_W), F32), rows], axis=0)
            newt_ref[g, i * ATT_W:(i + 1) * ATT_W, :] = padded.T


def _shift_cache_rows(c_ref, newt_ref, u_ref, g, r0, tq):
    win = GROUPS[g][0]
    lane = lax.broadcasted_iota(jnp.int32, (SHIFT_ROWS, LANES), 1)
    rows = pl.ds(r0, SHIFT_ROWS)
    rolled = pltpu.roll(c_ref[rows, :], win - tq, axis=1)
    if win > LANES:
        u_ref[rows, 0:win - LANES] = rolled[:, 0:win - LANES]
    u_ref[rows, win - LANES:] = jnp.where(lane >= LANES - tq, newt_ref[g, rows, :], rolled[:, win - LANES:])


def _conv_branch_kernel(x_ref, w3_ref, wga_ref, b3_ref, bga_ref, cw_ref, cb_ref, lg_ref, lb_ref, wa_ref,
                        new_ref, c1_ref, c2_ref, c3_ref,
                        gapa_ref, cst_ref, u1_ref, u2_ref, u3_ref, uh_ref, cv_ref, ca_ref, newt_ref,
                        *, tm, ns, tq):
    s = pl.program_id(1)
    ch = PROJ_COLS
    _stage_new_rows(new_ref, newt_ref, tq)

    @pl.when(s == 0)
    def _():
        uh_ref[0:HIST, :] = jnp.zeros((HIST, CONV_CH), F32)

    xb = x_ref[...].astype(BF16)

    def proj(c0, n):
        return _dot(xb, w3_ref[:, c0:c0 + n]) + b3_ref[:, c0:c0 + n]

    for c in range(0, CONV_CH, ch):
        a = proj(O_AVAL + c, ch)
        gt = proj(O_AGATE + c, ch)
        uh_ref[HIST:HIST + tm, c:c + ch] = a * _sigmoid(gt)

    rc = CONV_ROWS

    def conv_rows(i, carry):
        base = pl.multiple_of(HIST - 8 + i * rc, 8)
        for c in range(0, CONV_CH, ch):
            run = None
            for b in range(7, -1, -1):
                part = None
                for a in range((CONV_WIDTH - 1 - b) // 8 + 1):
                    j = CONV_WIDTH - 1 - (8 * a + b)
                    wj = jnp.concatenate([cw_ref[j * 8:(j + 1) * 8, c:c + ch]] * (rc // 8 + 1), axis=0)
                    term = uh_ref[pl.ds(base - 8 * a, rc + 8), c:c + ch] * wj
                    part = term if part is None else part + term
                run = part if run is None else part + pltpu.roll(run, 1, axis=0)
            cv_ref[pl.ds(pl.multiple_of(i * rc, rc), rc), c:c + ch] = run[8:, :] + cb_ref[:, c:c + ch]
        for g, (c_ref, u_ref) in enumerate(((c1_ref, u1_ref), (c2_ref, u2_ref), (c3_ref, u3_ref))):
            for pair in range(c_ref.shape[0] // ATT_W):
                r0 = pl.multiple_of(pair * ATT_W + i * SHIFT_ROWS, SHIFT_ROWS)
                _shift_cache_rows(c_ref, newt_ref, u_ref, g, r0, tq)
        return carry

    assert (tm // rc) * SHIFT_ROWS == ATT_W
    lax.fori_loop(0, tm // rc, conv_rows, 0)

    @pl.when(s == ns - 1)
    def _():
        cst_ref[...] = uh_ref[tm:tm + HIST, :]

    uh_ref[0:HIST, :] = uh_ref[tm:tm + HIST, :]

    x = cv_ref[...]
    mu = jnp.mean(x, axis=-1, keepdims=True)
    xc = x - mu
    var = jnp.mean(xc * xc, axis=-1, keepdims=True)
    rstd = lax.rsqrt(var + LN_EPS)
    for c in range(0, CONV_CH, ch):
        y = (cv_ref[:, c:c + ch] - mu) * rstd * lg_ref[:, c:c + ch] + lb_ref[:, c:c + ch]
        za = proj(O_ZA + c, ch)
        ca_ref[:, c:c + ch] = (_silu(y) * _silu(za)).astype(BF16)

    ca = ca_ref[...]
    for c in range(0, D_MODEL, ch):
        pa = _dot(ca, wa_ref[:, c:c + ch])
        ga = _dot(xb, wga_ref[:, c:c + ch]) + bga_ref[:, c:c + ch]
        gapa_ref[:, c:c + ch] = (_sigmoid(ga) * pa).astype(BF16)


def _conv_branch_prompt(x, wbf, bsc, cw8, conv_b, lg, lb, wa_bf, kv_new, caches_t, tm):
    B, S, _ = x.shape
    ns = S // tm
    nb, _, tq, _ = kv_new.shape
    pairs = 2 * nb
    assert pairs % (B * ns) == 0
    ppb = pairs // (B * ns)
    kern = functools.partial(_conv_branch_kernel, tm=tm, ns=ns, tq=tq)
    cache_specs = [pl.BlockSpec((ppb * ATT_W, win), lambda b, s: (b * ns + s, 0)) for win, _ in GROUPS]
    cache_rows = [c.reshape(pairs * ATT_W, c.shape[-1]) for c in caches_t]
    return pl.pallas_call(
        kern,
        grid=(B, ns),
        in_specs=[
            pl.BlockSpec((None, tm, D_MODEL), lambda b, s: (b, s, 0)),
            _const_spec((D_MODEL, 3 * CONV_CH), (0, 0)),
            _const_spec((D_MODEL, D_MODEL), (0, O_GA // D_MODEL)),
            _const_spec((1, 3 * CONV_CH), (0, 0)),
            _const_spec((1, D_MODEL), (0, O_GA // D_MODEL)),
            _const_spec((CONV_WIDTH * 8, CONV_CH), (0, 0)),
            _const_spec((1, CONV_CH), (0, 0)),
            _const_spec((1, CONV_CH), (0, 0)),
            _const_spec((1, CONV_CH), (0, 0)),
            _const_spec((CONV_CH, D_MODEL), (0, 0)),
            pl.BlockSpec((ppb, tq, QKV_W), lambda b, s: (b * ns + s, 0, 0)),
        ] + cache_specs,
        out_specs=[
            pl.BlockSpec((None, tm, D_MODEL), lambda b, s: (b, s, 0)),
            pl.BlockSpec((None, HIST, CONV_CH), lambda b, s: (b, 0, 0)),
        ] + cache_specs,
        out_shape=[
            jax.ShapeDtypeStruct((B, S, D_MODEL), BF16),
            jax.ShapeDtypeStruct((B, HIST, CONV_CH), F32),
        ] + [jax.ShapeDtypeStruct(c.shape, F32) for c in cache_rows],
        scratch_shapes=[
            pltpu.VMEM((HIST + tm, CONV_CH), F32),
            pltpu.VMEM((tm, CONV_CH), F32),
            pltpu.VMEM((tm, CONV_CH), BF16),
            pltpu.VMEM((N_GROUPS, ppb * ATT_W, LANES), F32),
        ],
        compiler_params=pltpu.CompilerParams(
            dimension_semantics=("arbitrary", "arbitrary"), vmem_limit_bytes=VMEM_LIMIT),
        name="conv_branch_prompt",
    )(x, wbf, wbf, bsc, bsc, cw8, conv_b, lg, lb, wa_bf, kv_new.reshape(pairs, tq, QKV_W), *cache_rows)


N_SLAB = D_MODEL // LANES


N_SAMPLE_IN = 13
N_SAMPLE_SCRATCH = 5


def _qkv_kernel(x_ref, wq_ref, wk_ref, wv_ref, wzb_ref, wgb_ref, bq_ref, bk_ref, bv_ref, bzb_ref, bgb_ref,
                *rest, tm, ns, tq):
    sample_in = rest[:N_SAMPLE_IN]
    o1_ref, o2_ref, o3_ref, zb_ref, sgb_ref, so_ref = rest[N_SAMPLE_IN:N_SAMPLE_IN + 6]
    xs_ref = rest[N_SAMPLE_IN + 6]
    sample_scratch = rest[N_SAMPLE_IN + 7:]
    step = pl.program_id(0) * ns + pl.program_id(1)
    _sample_attn_step(step % 2, *sample_in, so_ref, *sample_scratch, tq=tq)
    xb = x_ref[...].astype(BF16)
    for sl in range(N_SLAB):
        xs_ref[sl] = x_ref[:, sl * LANES:(sl + 1) * LANES]
    outs = (o1_ref, o2_ref, o3_ref)
    for g, (_, dil) in enumerate(GROUPS):
        n = tm // dil
        if dil == 1:
            lhs = xb
        else:
            lhs = jnp.concatenate(
                [jnp.concatenate([xs_ref[sl, pl.ds(r, n, stride=dil), :] for sl in range(N_SLAB)], axis=1)
                 for r in range(dil)], axis=0).astype(BF16)
        c0 = g * ATT_W
        for part, (w_ref, b_ref) in enumerate(((wq_ref, bq_ref), (wk_ref, bk_ref), (wv_ref, bv_ref))):
            res = _dot(lhs, w_ref[:, c0:c0 + ATT_W]) + b_ref[:, c0:c0 + ATT_W]
            outs[g][:, :, part * ATT_W:(part + 1) * ATT_W] = res.astype(BF16).reshape(dil, n, ATT_W)
    zb = _dot(xb, wzb_ref[...]) + bzb_ref[...]
    zb_ref[...] = _silu(zb).astype(BF16)
    for c in range(0, D_MODEL, GATE_COLS):
        gb = _dot(xb, wgb_ref[:, c:c + GATE_COLS]) + bgb_ref[:, c:c + GATE_COLS]
        sgb_ref[:, c:c + GATE_COLS] = _sigmoid(gb).astype(BF16)


def _qkv_prompt(x, wbf, bsc, hs3, caches_t, tm):
    B, S, _ = x.shape
    ns = S // tm
    nb, tq, _ = hs3.shape
    assert B * ns == 2 * nb
    s_in, s_args, s_out_spec, s_out_shape, s_scratch = _sample_attn_operands(
        hs3, caches_t, nb, tq, lambda b, s: b * ns + s)
    assert len(s_in) == N_SAMPLE_IN and len(s_scratch) == N_SAMPLE_SCRATCH
    tok = lambda width: pl.BlockSpec((None, tm, width), lambda b, s: (b, s, 0))
    grp = lambda dil: pl.BlockSpec((None, dil, tm // dil, QKV_W), lambda b, s: (b, 0, s, 0))
    wspec = lambda width, off: _const_spec((D_MODEL, width), (0, off // width))
    bspec = lambda width, off: _const_spec((1, width), (0, off // width))
    return pl.pallas_call(
        functools.partial(_qkv_kernel, tm=tm, ns=ns, tq=tq),
        grid=(B, ns),
        in_specs=[
            tok(D_MODEL),
            wspec(QKV_W, O_Q), wspec(QKV_W, O_K), wspec(QKV_W, O_V), wspec(ATT_W, O_ZB), wspec(D_MODEL, O_GB),
            bspec(QKV_W, O_Q), bspec(QKV_W, O_K), bspec(QKV_W, O_V), bspec(ATT_W, O_ZB), bspec(D_MODEL, O_GB),
        ] + s_in,
        out_specs=[grp(d) for _, d in GROUPS] + [tok(ATT_W), tok(D_MODEL), s_out_spec],
        out_shape=[jax.ShapeDtypeStruct((B, d, S // d, QKV_W), BF16) for _, d in GROUPS] + [
            jax.ShapeDtypeStruct((B, S, ATT_W), BF16),
            jax.ShapeDtypeStruct((B, S, D_MODEL), BF16),
            s_out_shape,
        ],
        scratch_shapes=[pltpu.VMEM((N_SLAB, tm, LANES), F32)] + s_scratch,
        compiler_params=pltpu.CompilerParams(
            dimension_semantics=("arbitrary", "arbitrary"), vmem_limit_bytes=VMEM_LIMIT),
        name="qkv_prompt",
    )(x, wbf, wbf, wbf, wbf, wbf, bsc, bsc, bsc, bsc, bsc, *s_args)


def _kv_window_kernel(x_ref, wkt_ref, wvt_ref, bkt_ref, bvt_ref, o1_ref, o2_ref, o3_ref, *, tw, nw, keeps):
    j = pl.program_id(1)
    xb = x_ref[...].astype(BF16)
    for g, o_ref in enumerate((o1_ref, o2_ref, o3_ref)):
        rows = min(keeps[g], tw)
        nwt = keeps[g] // rows
        c0 = g * ATT_W

        @pl.when(j >= nw - nwt)
        def _(o_ref=o_ref, rows=rows, c0=c0):
            xw = xb[tw - rows:, :]
            o_ref[0] = _dot_nt(wkt_ref[c0:c0 + ATT_W, :], xw) + bkt_ref[c0:c0 + ATT_W, :]
            o_ref[1] = _dot_nt(wvt_ref[c0:c0 + ATT_W, :], xw) + bvt_ref[c0:c0 + ATT_W, :]


def _kv_window(x, wkt, wvt, bkt, bvt, tw):
    B, S, _ = x.shape
    keeps = tuple(min(w, S) for w, _ in GROUPS)
    nw = max(keeps) // tw
    first = S // tw - nw

    def out_spec(keep):
        rows = min(keep, tw)
        f = nw - keep // rows
        return pl.BlockSpec((None, 2, ATT_W, rows), lambda b, j: (b, 0, 0, jnp.maximum(j - f, 0)))

    return pl.pallas_call(
        functools.partial(_kv_window_kernel, tw=tw, nw=nw, keeps=keeps),
        grid=(B, nw),
        in_specs=[
            pl.BlockSpec((None, tw, D_MODEL), lambda b, j: (b, first + j, 0)),
            _const_spec((QKV_W, D_MODEL), (0, 0)), _const_spec((QKV_W, D_MODEL), (0, 0)),
            _const_spec((QKV_W, 1), (0, 0)), _const_spec((QKV_W, 1), (0, 0)),
        ],
        out_specs=[out_spec(k) for k in keeps],
        out_shape=[jax.ShapeDtypeStruct((B, 2, ATT_W, k), F32) for k in keeps],
        compiler_params=pltpu.CompilerParams(
            dimension_semantics=("arbitrary", "arbitrary"), vmem_limit_bytes=VMEM_LIMIT),
        name="kv_window",
    )(x, wkt, wvt, bkt, bvt)


def _prompt_bias_tables(dil):
    slopes = _alibi_slopes()
    i = np.arange(Q_BLK)[:, None]
    j = np.arange(2 * Q_BLK)[None, :]
    delta = i + Q_BLK - j
    band = (delta >= 0) & (delta <= Q_BLK)
    bias = -slopes[:, None, None] * (delta * dil).astype(np.float32)[None]
    rest = np.where(band[None], bias, np.float32(NEG)).astype(np.float32)
    first = np.where((band & (j >= Q_BLK))[None], bias, np.float32(NEG)).astype(np.float32)
    return np.stack([first, rest]), rest


def _attn_kernel(q_ref, kp_ref, kc_ref, vp_ref, vc_ref, b0_ref, br_ref, o_ref, side_ref, *, nsub, rpb):
    lane = lax.broadcasted_iota(jnp.int32, (Q_BLK, LANES), 1)
    lo = lane < HEAD_DIM
    m_lo = lo.astype(F32).astype(BF16)
    m_hi = (1.0 - lo.astype(F32)).astype(BF16)
    ones = jnp.ones((2 * Q_BLK, LANES), BF16)
    for rr, i in ((rr, i) for rr in range(rpb) for i in range(nsub)):
        r0 = i * Q_BLK
        side = jnp.zeros((Q_BLK, LANES), F32)
        for hp in range(N_SLOTS // 2):
            l0 = hp * LANES
            q2 = q_ref[rr, r0:r0 + Q_BLK, l0:l0 + LANES]
            if i == 0:
                k2 = jnp.concatenate([kp_ref[rr, :, l0:l0 + LANES], kc_ref[rr, 0:Q_BLK, l0:l0 + LANES]], axis=0)
                v2 = jnp.concatenate([vp_ref[rr, :, l0:l0 + LANES], vc_ref[rr, 0:Q_BLK, l0:l0 + LANES]], axis=0)
                bref = b0_ref
            else:
                k2 = kc_ref[rr, r0 - Q_BLK:r0 + Q_BLK, l0:l0 + LANES]
                v2 = vc_ref[rr, r0 - Q_BLK:r0 + Q_BLK, l0:l0 + LANES]
                bref = br_ref
            outs = []
            for par in range(2):
                h = 2 * hp + par
                qm = q2 * (m_lo if par == 0 else m_hi)
                sc = _dot_nt(qm, k2) + bref[h]
                mx = jnp.max(sc, axis=1, keepdims=True)
                p = jnp.exp(sc - mx).astype(BF16)
                oe = _dot(p, jnp.concatenate([v2, ones], axis=1))
                outs.append(oe[:, 0:LANES])
                side = jnp.where(lane == h, mx, side)
                side = jnp.where(lane == N_SLOTS + h, oe[:, LANES:], side)
            o_ref[rr, r0:r0 + Q_BLK, l0:l0 + LANES] = jnp.where(lo, outs[0], outs[1]).astype(BF16)
        side_ref[rr, r0:r0 + Q_BLK, :] = side


ATTN_STEP_ROWS = 4096


def _attn_prompt(qkv, g):
    B, dil, L, _ = qkv.shape
    qb = min(ATTN_STEP_ROWS, L)
    nq = L // qb
    nsub = qb // Q_BLK
    rpb = min(dil, max(1, ATTN_STEP_ROWS // qb))
    b0, br = _prompt_bias_tables(dil)
    cur = lambda part: pl.BlockSpec((None, rpb, qb, ATT_W), lambda b, r, n: (b, r, n, part))
    prev = lambda part: pl.BlockSpec((None, rpb, Q_BLK, ATT_W),
                                     lambda b, r, n: (b, r, jnp.maximum(n * nsub - 1, 0), part))
    return pl.pallas_call(
        functools.partial(_attn_kernel, nsub=nsub, rpb=rpb),
        grid=(B, dil // rpb, nq),
        in_specs=[
            cur(0), prev(1), cur(1), prev(2), cur(2),
            pl.BlockSpec((None, N_SLOTS, Q_BLK, 2 * Q_BLK), lambda b, r, n: (jnp.minimum(n, 1), 0, 0, 0)),
            _const_spec((N_SLOTS, Q_BLK, 2 * Q_BLK), (0, 0, 0)),
        ],
        out_specs=[
            pl.BlockSpec((None, rpb, qb, ATT_W), lambda b, r, n: (b, r, n, 0)),
            pl.BlockSpec((None, rpb, qb, LANES), lambda b, r, n: (b, r, n, 0)),
        ],
        out_shape=[
            jax.ShapeDtypeStruct((B, dil, L, ATT_W), BF16),
            jax.ShapeDtypeStruct((B, dil, L, LANES), F32),
        ],
        compiler_params=pltpu.CompilerParams(
            dimension_semantics=("arbitrary", "arbitrary", "arbitrary"), vmem_limit_bytes=VMEM_LIMIT),
        name=f"attn_prompt_g{g}",
    )(qkv, qkv, qkv, qkv, qkv, jnp.asarray(b0), jnp.asarray(br))


def _merge_tail(x, gapa, sgb, zb, attn, wb_ref, wo_ref, g_ref, bt_ref):
    ab = (attn * zb.astype(F32)).astype(BF16)
    pb = _dot(ab, wb_ref[...])
    m = gapa.astype(F32) + sgb.astype(F32) * pb
    o = _dot(m.astype(BF16), wo_ref[...])
    return _layer_norm(ALPHA * x + o, g_ref[...], bt_ref[...])


def _merge_groups_kernel(x_ref, gapa_ref, sgb_ref, zb_ref, o1_ref, o2_ref, o3_ref, s1_ref, s2_ref, s3_ref,
                         ex_ref, wb_ref, wo_ref, g_ref, bt_ref, y_ref, nat_ref, nats_ref, *, tm):
    nsl = ATT_W // LANES
    for gi, (o_ref, s_ref) in enumerate(((o2_ref, s2_ref), (o3_ref, s3_ref))):
        dil = GROUPS[gi + 1][1]
        n = tm // dil
        for r in range(dil):
            blk = o_ref[r].astype(F32)
            for sl in range(nsl):
                nat_ref[gi, sl, pl.ds(r, n, stride=dil), :] = blk[:, sl * LANES:(sl + 1) * LANES]
            nats_ref[gi, pl.ds(r, n, stride=dil), :] = s_ref[r]
    sides = (s1_ref[...], nats_ref[0], nats_ref[1])
    mx = jnp.maximum(jnp.maximum(sides[0], sides[1]), sides[2])
    es = [jnp.exp(t - mx) for t in sides]
    dens = [pltpu.roll(t, LANES - N_SLOTS, axis=1) for t in sides]
    total = es[0] * dens[0] + es[1] * dens[1] + es[2] * dens[2]
    head_lane = lax.broadcasted_iota(jnp.int32, total.shape, 1) < N_SLOTS
    inv = jnp.where(head_lane, 1.0 / total, 0.0)
    ex = ex_ref[...]
    w1, w2, w3 = (_dot((e * inv).astype(BF16), ex) for e in es)
    slabs = []
    for sl in range(nsl):
        cs = slice(sl * LANES, (sl + 1) * LANES)
        slabs.append(w1[:, cs] * o1_ref[:, cs].astype(F32) + w2[:, cs] * nat_ref[0, sl] + w3[:, cs] * nat_ref[1, sl])
    attn = jnp.concatenate(slabs, axis=1)
    y_ref[...] = _merge_tail(x_ref[...], gapa_ref[...], sgb_ref[...], zb_ref[...], attn,
                             wb_ref, wo_ref, g_ref, bt_ref)


def _merge_plain_kernel(x_ref, gapa_ref, sgb_ref, zb_ref, attn_ref, wb_ref, wo_ref, g_ref, bt_ref, y_ref):
    y_ref[...] = _merge_tail(x_ref[...], gapa_ref[...], sgb_ref[...], zb_ref[...], attn_ref[...],
                             wb_ref, wo_ref, g_ref, bt_ref)


def _head_expand_matrix():
    ex = np.zeros((LANES, ATT_W), np.float32)
    for h in range(N_SLOTS):
        ex[h, h * HEAD_DIM:(h + 1) * HEAD_DIM] = 1.0
    return ex


def _merge_weight_specs():
    return [_const_spec((ATT_W, D_MODEL), (0, 0)), _const_spec((D_MODEL, D_MODEL), (0, 0)),
            _const_spec((1, D_MODEL), (0, 0)), _const_spec((1, D_MODEL), (0, 0))]


MERGE_ROWS = 1024


def _merge_groups(x, gapa, sgb, zb, parts, wb_bf, wo_bf, ln_g, ln_b, tm):
    B, S, _ = x.shape
    assert S % tm == 0 and tm % (GROUPS[-1][1] * 8) == 0
    tok = lambda width: pl.BlockSpec((None, tm, width), lambda b, s: (b, s, 0))
    grp = lambda dil, width: pl.BlockSpec((None, dil, tm // dil, width), lambda b, s: (b, 0, s, 0))
    (o1, s1), (o2, s2), (o3, s3) = parts
    d2, d3 = GROUPS[1][1], GROUPS[2][1]
    return pl.pallas_call(
        functools.partial(_merge_groups_kernel, tm=tm),
        grid=(B, S // tm),
        in_specs=[tok(D_MODEL), tok(D_MODEL), tok(D_MODEL), tok(ATT_W),
                  tok(ATT_W), grp(d2, ATT_W), grp(d3, ATT_W), tok(LANES), grp(d2, LANES), grp(d3, LANES),
                  _const_spec((LANES, ATT_W), (0, 0))] + _merge_weight_specs(),
        out_specs=tok(D_MODEL),
        out_shape=jax.ShapeDtypeStruct((B, S, D_MODEL), F32),
        scratch_shapes=[pltpu.VMEM((2, ATT_W // LANES, tm, LANES), F32), pltpu.VMEM((2, tm, LANES), F32)],
        compiler_params=pltpu.CompilerParams(
            dimension_semantics=("arbitrary", "arbitrary"), vmem_limit_bytes=VMEM_LIMIT),
        name="merge_groups",
    )(x, gapa, sgb, zb, o1.reshape(B, S, ATT_W), o2, o3, s1.reshape(B, S, LANES), s2, s3,
      jnp.asarray(_head_expand_matrix(), BF16), wb_bf, wo_bf, ln_g, ln_b)


def _merge_plain(x2, gapa, sgb, zb, attn, wb_bf, wo_bf, ln_g, ln_b):
    T = x2.shape[0]
    tok = lambda width: pl.BlockSpec((T, width), lambda t: (0, 0))
    return pl.pallas_call(
        _merge_plain_kernel,
        grid=(1,),
        in_specs=[tok(D_MODEL), tok(D_MODEL), tok(D_MODEL), tok(ATT_W), tok(ATT_W)] + _merge_weight_specs(),
        out_specs=tok(D_MODEL),
        out_shape=jax.ShapeDtypeStruct((T, D_MODEL), F32),
        compiler_params=pltpu.CompilerParams(dimension_semantics=("arbitrary",), vmem_limit_bytes=VMEM_LIMIT),
        name="merge_plain",
    )(x2, gapa, sgb, zb, attn, wb_bf, wo_bf, ln_g, ln_b)


def _sample_proj_kernel(x_ref, w_ref, b_ref, h_ref):
    h_ref[...] = _dot(x_ref[...].astype(BF16), w_ref[...]) + b_ref[...]


def _sample_proj(xs2, wbf, bsc):
    T = xs2.shape[0]
    cw = SAMPLE_PROJ_COLS
    return pl.pallas_call(
        _sample_proj_kernel,
        grid=(IN_W // cw,),
        in_specs=[
            pl.BlockSpec((T, D_MODEL), lambda j: (0, 0)),
            pl.BlockSpec((D_MODEL, cw), lambda j: (0, j)),
            pl.BlockSpec((1, cw), lambda j: (0, j)),
        ],
        out_specs=pl.BlockSpec((T, cw), lambda j: (0, j)),
        out_shape=jax.ShapeDtypeStruct((T, IN_W), F32),
        compiler_params=pltpu.CompilerParams(dimension_semantics=("arbitrary",), vmem_limit_bytes=VMEM_LIMIT),
        name="sample_proj",
    )(xs2, wbf, bsc)


def _sample_conv_kernel(h3_ref, hga_ref, hzb_ref, hgb_ref, st_ref, cw_ref, cb_ref, lg_ref, lb_ref, wa_ref,
                        gapa_ref, zb_ref, sgb_ref, cst_ref, hist_ref, slab_ref, cvt_ref, *, nb, tq):
    nh = CONV_WIDTH - 1
    a = h3_ref[:, O_AVAL:O_AVAL + CONV_CH]
    gt = h3_ref[:, O_AGATE:O_AGATE + CONV_CH]
    u = a * _sigmoid(gt)
    for sl in range(N_SLAB):
        slab_ref[sl] = u[:, sl * LANES:(sl + 1) * LANES]
    hist_ref[0:nh] = st_ref[...]
    for t in range(tq):
        hist_ref[nh + t] = jnp.concatenate(
            [slab_ref[sl, pl.ds(t, nb, stride=tq), :] for sl in range(N_SLAB)], axis=1)

    def conv_step(t, carry):
        acc = jnp.zeros((nb, CONV_CH), F32)
        for j in range(CONV_WIDTH):
            wj = jnp.concatenate([cw_ref[j * 8:(j + 1) * 8, :]] * (nb // 8), axis=0)
            acc = acc + hist_ref[t + j] * wj
        cvt_ref[t] = acc + cb_ref[...]
        return carry

    lax.fori_loop(0, tq, conv_step, 0)
    for t in range(tq):
        cvt = cvt_ref[t]
        for sl in range(N_SLAB):
            slab_ref[sl, pl.ds(t, nb, stride=tq), :] = cvt[:, sl * LANES:(sl + 1) * LANES]
    cst_ref[...] = hist_ref[tq:tq + nh]

    cv = jnp.concatenate([slab_ref[sl] for sl in range(N_SLAB)], axis=1)
    y = _layer_norm(cv, lg_ref[...], lb_ref[...])
    ca = (_silu(y) * _silu(h3_ref[:, O_ZA:O_ZA + CONV_CH])).astype(BF16)
    gapa_ref[...] = (_sigmoid(hga_ref[...]) * _dot(ca, wa_ref[...])).astype(BF16)
    zb_ref[...] = _silu(hzb_ref[...]).astype(BF16)
    sgb_ref[...] = _sigmoid(hgb_ref[...]).astype(BF16)


def _sample_conv_branch(hs, state_t, cw8, conv_b, lg, lb, wa_bf, nb, tq):
    T = nb * tq
    nh = CONV_WIDTH - 1
    col = lambda width, off: pl.BlockSpec((T, width), lambda i: (0, off // width))
    full = lambda shape: pl.BlockSpec(shape, lambda i: (0,) * len(shape))
    return pl.pallas_call(
        functools.partial(_sample_conv_kernel, nb=nb, tq=tq),
        grid=(1,),
        in_specs=[
            col(3 * CONV_CH, 0), col(D_MODEL, O_GA), col(ATT_W, O_ZB), col(D_MODEL, O_GB),
            full((nh, nb, CONV_CH)), full((CONV_WIDTH * 8, CONV_CH)),
            full((1, CONV_CH)), full((1, CONV_CH)), full((1, CONV_CH)), full((CONV_CH, D_MODEL)),
        ],
        out_specs=[full((T, D_MODEL)), full((T, ATT_W)), full((T, D_MODEL)), full((nh, nb, CONV_CH))],
        out_shape=[
            jax.ShapeDtypeStruct((T, D_MODEL), BF16),
            jax.ShapeDtypeStruct((T, ATT_W), BF16),
            jax.ShapeDtypeStruct((T, D_MODEL), BF16),
            jax.ShapeDtypeStruct((nh, nb, CONV_CH), F32),
        ],
        scratch_shapes=[pltpu.VMEM((nh + tq, nb, CONV_CH), F32), pltpu.VMEM((N_SLAB, T, LANES), F32),
                        pltpu.VMEM((tq, nb, CONV_CH), F32)],
        compiler_params=pltpu.CompilerParams(dimension_semantics=("arbitrary",), vmem_limit_bytes=VMEM_LIMIT),
        name="sample_conv_branch",
    )(hs, hs, hs, hs, state_t, cw8, conv_b, lg, lb, wa_bf)


def _sample_bias_tables(g, tq):
    win, dil = GROUPS[g]
    slopes = _alibi_slopes()
    tabc = np.full((N_SLOTS * tq, win), NEG, np.float32)
    tabn = np.full((N_SLOTS * tq, tq), NEG, np.float32)
    pos = np.arange(win)
    for h in range(N_SLOTS):
        for t in range(tq):
            row = h * tq + t
            dist = win + t - pos
            ok = (dist % dil == 0) & (dist // dil <= win // dil)
            tabc[row, ok] = -slopes[h] * dist[ok].astype(np.float32)
            for t2 in range(t + 1):
                if (t - t2) % dil == 0:
                    tabn[row, t2] = -slopes[h] * np.float32(t - t2)
    return tabc, tabn


def _sample_attn_step(kv, q_ref, k_ref, v_ref, c1_ref, c2_ref, c3_ref, tc1_ref, tc2_ref, tc3_ref,
                      tn1_ref, tn2_ref, tn3_ref, hm_ref, o_ref,
                      p1_ref, p2_ref, p3_ref, pn_ref, den_ref, *, tq):
    caches = (c1_ref, c2_ref, c3_ref)
    probs = (p1_ref, p2_ref, p3_ref)
    tabc = (tc1_ref, tc2_ref, tc3_ref)
    tabn = (tn1_ref, tn2_ref, tn3_ref)
    hm = hm_ref[...]

    @pl.when(kv == 0)
    def _():
        scores, mx = [], None
        for g in range(N_GROUPS):
            c0 = g * ATT_W
            qbd = (jnp.concatenate([q_ref[:, c0:c0 + ATT_W]] * N_SLOTS, axis=0) * hm).astype(BF16)
            sc = _dot(qbd, caches[g][...].astype(BF16)) + tabc[g][...]
            sn = _dot_nt(qbd, k_ref[:, c0:c0 + ATT_W].astype(BF16)) + tabn[g][...]
            m = jnp.maximum(jnp.max(sc, axis=1, keepdims=True), jnp.max(sn, axis=1, keepdims=True))
            mx = m if mx is None else jnp.maximum(mx, m)
            scores.append((sc, sn))
        den = jnp.zeros((N_SLOTS * tq, 1), F32)
        for g in range(N_GROUPS):
            sc, sn = scores[g]
            p = jnp.exp(sc - mx)
            pn = jnp.exp(sn - mx)
            den = den + jnp.sum(p, axis=1, keepdims=True) + jnp.sum(pn, axis=1, keepdims=True)
            probs[g][...] = p.astype(BF16)
            pn_ref[g] = pn
        den_ref[...] = den

    @pl.when(kv == 1)
    def _():
        acc = jnp.zeros((N_SLOTS * tq, ATT_W), F32)
        for g in range(N_GROUPS):
            c0 = g * ATT_W
            acc = acc + _dot_nt(probs[g][...], caches[g][...].astype(BF16))
            acc = acc + _dot(pn_ref[g].astype(BF16), v_ref[:, c0:c0 + ATT_W].astype(BF16))
        out = (acc / den_ref[...]) * hm
        o_ref[...] = jnp.sum(out.reshape(N_SLOTS, tq, ATT_W), axis=0)


def _sample_attn_operands(hs3, caches_t, nb, tq, step_of):
    seq = lambda *ids: step_of(*ids) // 2
    half = lambda *ids: step_of(*ids) % 2
    cspecs, tcs, tns = [], [], []
    for g, (win, _) in enumerate(GROUPS):
        cspecs.append(pl.BlockSpec((None, None, ATT_W, win), lambda *ids: (seq(*ids), half(*ids), 0, 0)))
        tc, tn = _sample_bias_tables(g, tq)
        tcs.append(jnp.asarray(tc))
        tns.append(jnp.asarray(tn))
    hm = np.zeros((N_SLOTS * tq, ATT_W), np.float32)
    for h in range(N_SLOTS):
        hm[h * tq:(h + 1) * tq, h * HEAD_DIM:(h + 1) * HEAD_DIM] = 1.0
    qkv = lambda off: pl.BlockSpec((None, tq, QKV_W), lambda *ids: (seq(*ids), 0, off // QKV_W))
    const = lambda a: _const_spec(a.shape, (0,) * a.ndim)
    nrow = N_SLOTS * tq
    in_specs = ([qkv(O_Q), qkv(O_K), qkv(O_V)] + cspecs + [const(t) for t in tcs] + [const(t) for t in tns]
                + [_const_spec(hm.shape, (0, 0))])
    args = [hs3, hs3, hs3, *caches_t, *tcs, *tns, jnp.asarray(hm)]
    out_spec = pl.BlockSpec((None, tq, ATT_W), lambda *ids: (seq(*ids), 0, 0))
    out_shape = jax.ShapeDtypeStruct((nb, tq, ATT_W), F32)
    scratch = ([pltpu.VMEM((nrow, win), BF16) for win, _ in GROUPS]
               + [pltpu.VMEM((N_GROUPS, nrow, tq), F32), pltpu.VMEM((nrow, 1), F32)])
    return in_specs, args, out_spec, out_shape, scratch


def _to_feature_major(c):
    n, w = c.shape[0], c.shape[1]
    return jnp.transpose(c, (0, 2, 3, 4, 1)).reshape(n, 2, ATT_W, w)


def _from_feature_major(c):
    n, _, _, w = c.shape
    return jnp.transpose(c.reshape(n, 2, N_SLOTS, HEAD_DIM, w), (0, 4, 1, 2, 3))


def kernel(x_prompt, x_sample, cache_kv_w128, cache_kv_w512, cache_kv_w2048, state_conv, w_in, b_in, conv_w, conv_b,
           conv_ln_g, conv_ln_b, w_a, w_b, w_out, ln_g, ln_b):
    B, S, _ = x_prompt.shape
    nb, tq, _ = x_sample.shape
    caches = (cache_kv_w128, cache_kv_w512, cache_kv_w2048)
    tm = PROMPT_ROWS
    for (win, dil), c in zip(GROUPS, caches):
        assert c.shape[1] == win and S % (dil * Q_BLK) == 0 and S >= win and tm % (dil * 16) == 0
    assert tq == 8 and nb % 8 == 0 and state_conv.shape[1] == CONV_WIDTH - 1 and S % tm == 0

    col_scale = np.ones((IN_W,), np.float32)
    col_scale[O_Q:O_K] = HEAD_DIM ** -0.5
    wbf = (w_in * col_scale).astype(BF16)
    b1 = b_in * col_scale
    bsc = b1.reshape(1, IN_W)
    wkvt = lax.optimization_barrier(w_in[:, O_K:O_ZB]).T.astype(BF16)
    wkt, wvt = wkvt[:QKV_W], wkvt[QKV_W:]
    bkt, bvt = b1[O_K:O_V].reshape(QKV_W, 1), b1[O_V:O_ZB].reshape(QKV_W, 1)
    cw8 = jnp.broadcast_to(conv_w[:, None, :], (CONV_WIDTH, 8, CONV_CH)).reshape(CONV_WIDTH * 8, CONV_CH)
    cb, clg, clb = (t.reshape(1, CONV_CH) for t in (conv_b, conv_ln_g, conv_ln_b))
    lg, lb = ln_g.reshape(1, D_MODEL), ln_b.reshape(1, D_MODEL)
    wa_bf, wb_bf, wo_bf = w_a.astype(BF16), w_b.astype(BF16), w_out.astype(BF16)

    Ts = nb * tq
    hs = _sample_proj(x_sample.reshape(Ts, D_MODEL), wbf, bsc)
    hs3 = hs.reshape(nb, tq, IN_W)
    kv_new = jnp.stack([hs3[:, :, O_K:O_V], hs3[:, :, O_V:O_ZB]], axis=1)
    caches_t = [_to_feature_major(c) for c in caches]

    gapa, cst_p, *kvs_rows = _conv_branch_prompt(x_prompt, wbf, bsc, cw8, cb, clg, clb, wa_bf, kv_new, caches_t, tm)
    qkv1, qkv2, qkv3, zb, sgb, attn_s = _qkv_prompt(x_prompt, wbf, bsc, hs3, caches_t, tm)
    kvt = _kv_window(x_prompt, wkt, wvt, bkt, bvt, KV_WINDOW_ROWS)
    parts = [_attn_prompt(t, g) for g, t in enumerate((qkv1, qkv2, qkv3))]
    y_p = _merge_groups(x_prompt, gapa, sgb, zb, parts, wb_bf, wo_bf, lg, lb, MERGE_ROWS)
    kv_p = [_from_feature_major(t) for t in kvt]
    conv_p = cst_p[:, HOFF:, :]

    state_t = jnp.transpose(state_conv, (1, 0, 2))
    gapa_s, zb_s, sgb_s, cst_t = _sample_conv_branch(hs, state_t, cw8, cb, clg, clb, wa_bf, nb, tq)
    y_s = _merge_plain(x_sample.reshape(Ts, D_MODEL), gapa_s, sgb_s, zb_s, attn_s.reshape(Ts, ATT_W),
                       wb_bf, wo_bf, lg, lb).reshape(nb, tq, D_MODEL)
    kv_s = [_from_feature_major(t.reshape(nb, 2, ATT_W, t.shape[-1])) for t in kvs_rows]
    conv_s = jnp.transpose(cst_t, (1, 0, 2))
    return (y_p, y_s, kv_p[0], kv_p[1], kv_p[2], conv_p, kv_s[0], kv_s[1], kv_s[2], conv_s)
```

```python
import functools

import numpy as np
import jax
import jax.numpy as jnp
from jax import lax
from jax.experimental import pallas as pl
from jax.experimental.pallas import tpu as pltpu

F32 = jnp.float32
BF16 = jnp.bfloat16

D_MODEL = 1024
CONV_CH = 1024
CONV_WIDTH = 31
N_SLOTS = 8
HEAD_DIM = 64
GROUPS = ((128, 1), (512, 4), (2048, 16))
N_GROUPS = len(GROUPS)
ATT_W = N_SLOTS * HEAD_DIM
QKV_W = N_GROUPS * ATT_W
Q_BLK = 128
DEPTH = 1
ALPHA = (2.0 * DEPTH) ** 0.25
LN_EPS = 1e-5
NEG = -1e30

O_AVAL, O_AGATE, O_ZA = 0, CONV_CH, 2 * CONV_CH
O_Q = 3 * CONV_CH
O_K = O_Q + QKV_W
O_V = O_K + QKV_W
O_ZB = O_V + QKV_W
O_GA = O_ZB + ATT_W
O_GB = O_GA + D_MODEL
IN_W = O_GB + D_MODEL

HIST = 32
HOFF = HIST - (CONV_WIDTH - 1)
LANES = 128
VMEM_LIMIT = 56 * 1024 * 1024
PROMPT_ROWS = 512
PROJ_COLS = 256
CONV_ROWS = 64
GATE_COLS = 512
SAMPLE_PROJ_COLS = 2048
KV_WINDOW_ROWS = 1024


def _dot(a, b):
    return jnp.dot(a, b, preferred_element_type=F32)


def _dot_nt(a, b):
    return lax.dot_general(a, b, (((1,), (1,)), ((), ())), preferred_element_type=F32)


def _sigmoid(x):
    return 0.5 * jnp.tanh(0.5 * x) + 0.5


def _silu(x):
    h = 0.5 * x
    return h * jnp.tanh(h) + h


def _layer_norm(x, g, b):
    mu = jnp.mean(x, axis=-1, keepdims=True)
    xc = x - mu
    var = jnp.mean(xc * xc, axis=-1, keepdims=True)
    return xc * lax.rsqrt(var + LN_EPS) * g + b


SMALL_BLOCK_ELEMS = 512 * 1024


def _const_spec(shape, index):
    if int(np.prod(shape)) <= SMALL_BLOCK_ELEMS:
        return pl.BlockSpec(shape, lambda *_: index)
    return pl.BlockSpec(shape, lambda *_: index, pipeline_mode=pl.Buffered(1))


def _alibi_slopes():
    return (2.0 ** (-8.0 * np.arange(1, N_SLOTS + 1, dtype=np.float64) / N_SLOTS)).astype(np.float32)


SHIFT_ROWS = 64


def _stage_new_rows(new_ref, newt_ref, tq):
    for g in range(N_GROUPS):
        for i in range(new_ref.shape[0]):
            rows = new_ref[i, :, g * ATT_W:(g + 1) * ATT_W]
            padded = jnp.concatenate([jnp.zeros((LANES - tq, ATT_W), F32), rows], axis=0)
            newt_ref[g, i * ATT_W:(i + 1) * ATT_W, :] = padded.T


def _shift_cache_rows(c_ref, newt_ref, u_ref, g, r0, tq):
    win = GROUPS[g][0]
    lane = lax.broadcasted_iota(jnp.int32, (SHIFT_ROWS, LANES), 1)
    rows = pl.ds(r0, SHIFT_ROWS)
    rolled = pltpu.roll(c_ref[rows, :], win - tq, axis=1)
    if win > LANES:
        u_ref[rows, 0:win - LANES] = rolled[:, 0:win - LANES]
    u_ref[rows, win - LANES:] = jnp.where(lane >= LANES - tq, newt_ref[g, rows, :], rolled[:, win - LANES:])


def _conv_branch_kernel(x_ref, w3_ref, wga_ref, b3_ref, bga_ref, cw_ref, cb_ref, lg_ref, lb_ref, wa_ref,
                        new_ref, c1_ref, c2_ref, c3_ref,
                        gapa_ref, cst_ref, u1_ref, u2_ref, u3_ref, uh_ref, cv_ref, ca_ref, newt_ref,
                        *, tm, ns, tq):
    s = pl.program_id(1)
    ch = PROJ_COLS
    _stage_new_rows(new_ref, newt_ref, tq)

    @pl.when(s == 0)
    def _():
        uh_ref[0:HIST, :] = jnp.zeros((HIST, CONV_CH), F32)

    xb = x_ref[...].astype(BF16)

    def proj(c0, n):
        return _dot(xb, w3_ref[:, c0:c0 + n]) + b3_ref[:, c0:c0 + n]

    for c in range(0, CONV_CH, ch):
        a = proj(O_AVAL + c, ch)
        gt = proj(O_AGATE + c, ch)
        uh_ref[HIST:HIST + tm, c:c + ch] = a * _sigmoid(gt)

    rc = CONV_ROWS

    def conv_rows(i, carry):
        base = pl.multiple_of(HIST - 8 + i * rc, 8)
        for c in range(0, CONV_CH, ch):
            run = None
            for b in range(7, -1, -1):
                part = None
                for a in range((CONV_WIDTH - 1 - b) // 8 + 1):
                    j = CONV_WIDTH - 1 - (8 * a + b)
                    wj = jnp.concatenate([cw_ref[j * 8:(j + 1) * 8, c:c + ch]] * (rc // 8 + 1), axis=0)
                    term = uh_ref[pl.ds(base - 8 * a, rc + 8), c:c + ch] * wj
                    part = term if part is None else part + term
                run = part if run is None else part + pltpu.roll(run, 1, axis=0)
            cv_ref[pl.ds(pl.multiple_of(i * rc, rc), rc), c:c + ch] = run[8:, :] + cb_ref[:, c:c + ch]
        for g, (c_ref, u_ref) in enumerate(((c1_ref, u1_ref), (c2_ref, u2_ref), (c3_ref, u3_ref))):
            for pair in range(c_ref.shape[0] // ATT_W):
                r0 = pl.multiple_of(pair * ATT_W + i * SHIFT_ROWS, SHIFT_ROWS)
                _shift_cache_rows(c_ref, newt_ref, u_ref, g, r0, tq)
        return carry

    assert (tm // rc) * SHIFT_ROWS == ATT_W
    lax.fori_loop(0, tm // rc, conv_rows, 0)

    @pl.when(s == ns - 1)
    def _():
        cst_ref[...] = uh_ref[tm:tm + HIST, :]

    uh_ref[0:HIST, :] = uh_ref[tm:tm + HIST, :]

    x = cv_ref[...]
    mu = jnp.mean(x, axis=-1, keepdims=True)
    xc = x - mu
    var = jnp.mean(xc * xc, axis=-1, keepdims=True)
    rstd = lax.rsqrt(var + LN_EPS)
    for c in range(0, CONV_CH, ch):
        y = (cv_ref[:, c:c + ch] - mu) * rstd * lg_ref[:, c:c + ch] + lb_ref[:, c:c + ch]
        za = proj(O_ZA + c, ch)
        ca_ref[:, c:c + ch] = (_silu(y) * _silu(za)).astype(BF16)

    ca = ca_ref[...]
    for c in range(0, D_MODEL, ch):
        pa = _dot(ca, wa_ref[:, c:c + ch])
        ga = _dot(xb, wga_ref[:, c:c + ch]) + bga_ref[:, c:c + ch]
        gapa_ref[:, c:c + ch] = (_sigmoid(ga) * pa).astype(BF16)


def _conv_branch_prompt(x, wbf, bsc, cw8, conv_b, lg, lb, wa_bf, kv_new, caches_t, tm):
    B, S, _ = x.shape
    ns = S // tm
    nb, _, tq, _ = kv_new.shape
    pairs = 2 * nb
    assert pairs % (B * ns) == 0
    ppb = pairs // (B * ns)
    kern = functools.partial(_conv_branch_kernel, tm=tm, ns=ns, tq=tq)
    cache_specs = [pl.BlockSpec((ppb * ATT_W, win), lambda b, s: (b * ns + s, 0)) for win, _ in GROUPS]
    cache_rows = [c.reshape(pairs * ATT_W, c.shape[-1]) for c in caches_t]
    return pl.pallas_call(
        kern,
        grid=(B, ns),
        in_specs=[
            pl.BlockSpec((None, tm, D_MODEL), lambda b, s: (b, s, 0)),
            _const_spec((D_MODEL, 3 * CONV_CH), (0, 0)),
            _const_spec((D_MODEL, D_MODEL), (0, O_GA // D_MODEL)),
            _const_spec((1, 3 * CONV_CH), (0, 0)),
            _const_spec((1, D_MODEL), (0, O_GA // D_MODEL)),
            _const_spec((CONV_WIDTH * 8, CONV_CH), (0, 0)),
            _const_spec((1, CONV_CH), (0, 0)),
            _const_spec((1, CONV_CH), (0, 0)),
            _const_spec((1, CONV_CH), (0, 0)),
            _const_spec((CONV_CH, D_MODEL), (0, 0)),
            pl.BlockSpec((ppb, tq, QKV_W), lambda b, s: (b * ns + s, 0, 0)),
        ] + cache_specs,
        out_specs=[
            pl.BlockSpec((None, tm, D_MODEL), lambda b, s: (b, s, 0)),
            pl.BlockSpec((None, HIST, CONV_CH), lambda b, s: (b, 0, 0)),
        ] + cache_specs,
        out_shape=[
            jax.ShapeDtypeStruct((B, S, D_MODEL), BF16),
            jax.ShapeDtypeStruct((B, HIST, CONV_CH), F32),
        ] + [jax.ShapeDtypeStruct(c.shape, F32) for c in cache_rows],
        scratch_shapes=[
            pltpu.VMEM((HIST + tm, CONV_CH), F32),
            pltpu.VMEM((tm, CONV_CH), F32),
            pltpu.VMEM((tm, CONV_CH), BF16),
            pltpu.VMEM((N_GROUPS, ppb * ATT_W, LANES), F32),
        ],
        compiler_params=pltpu.CompilerParams(
            dimension_semantics=("arbitrary", "arbitrary"), vmem_limit_bytes=VMEM_LIMIT),
        name="conv_branch_prompt",
    )(x, wbf, wbf, bsc, bsc, cw8, conv_b, lg, lb, wa_bf, kv_new.reshape(pairs, tq, QKV_W), *cache_rows)


N_SLAB = D_MODEL // LANES


N_SAMPLE_IN = 13
N_SAMPLE_SCRATCH = 5


def _qkv_kernel(x_ref, wq_ref, wk_ref, wv_ref, wzb_ref, wgb_ref, bq_ref, bk_ref, bv_ref, bzb_ref, bgb_ref,
                *rest, tm, ns, tq):
    sample_in = rest[:N_SAMPLE_IN]
    o1_ref, o2_ref, o3_ref, zb_ref, sgb_ref, so_ref = rest[N_SAMPLE_IN:N_SAMPLE_IN + 6]
    xs_ref = rest[N_SAMPLE_IN + 6]
    sample_scratch = rest[N_SAMPLE_IN + 7:]
    step = pl.program_id(0) * ns + pl.program_id(1)
    _sample_attn_step(step % 2, *sample_in, so_ref, *sample_scratch, tq=tq)
    xb = x_ref[...].astype(BF16)
    for sl in range(N_SLAB):
        xs_ref[sl] = x_ref[:, sl * LANES:(sl + 1) * LANES]
    outs = (o1_ref, o2_ref, o3_ref)
    for g, (_, dil) in enumerate(GROUPS):
        n = tm // dil
        if dil == 1:
            lhs = xb
        else:
            lhs = jnp.concatenate(
                [jnp.concatenate([xs_ref[sl, pl.ds(r, n, stride=dil), :] for sl in range(N_SLAB)], axis=1)
                 for r in range(dil)], axis=0).astype(BF16)
        c0 = g * ATT_W
        for part, (w_ref, b_ref) in enumerate(((wq_ref, bq_ref), (wk_ref, bk_ref), (wv_ref, bv_ref))):
            res = _dot(lhs, w_ref[:, c0:c0 + ATT_W]) + b_ref[:, c0:c0 + ATT_W]
            outs[g][:, :, part * ATT_W:(part + 1) * ATT_W] = res.astype(BF16).reshape(dil, n, ATT_W)
    zb = _dot(xb, wzb_ref[...]) + bzb_ref[...]
    zb_ref[...] = _silu(zb).astype(BF16)
    for c in range(0, D_MODEL, GATE_COLS):
        gb = _dot(xb, wgb_ref[:, c:c + GATE_COLS]) + bgb_ref[:, c:c + GATE_COLS]
        sgb_ref[:, c:c + GATE_COLS] = _sigmoid(gb).astype(BF16)


def _qkv_prompt(x, wbf, bsc, hs3, caches_t, tm):
    B, S, _ = x.shape
    ns = S // tm
    nb, tq, _ = hs3.shape
    assert B * ns == 2 * nb
    s_in, s_args, s_out_spec, s_out_shape, s_scratch = _sample_attn_operands(
        hs3, caches_t, nb, tq, lambda b, s: b * ns + s)
    assert len(s_in) == N_SAMPLE_IN and len(s_scratch) == N_SAMPLE_SCRATCH
    tok = lambda width: pl.BlockSpec((None, tm, width), lambda b, s: (b, s, 0))
    grp = lambda dil: pl.BlockSpec((None, dil, tm // dil, QKV_W), lambda b, s: (b, 0, s, 0))
    wspec = lambda width, off: _const_spec((D_MODEL, width), (0, off // width))
    bspec = lambda width, off: _const_spec((1, width), (0, off // width))
    return pl.pallas_call(
        functools.partial(_qkv_kernel, tm=tm, ns=ns, tq=tq),
        grid=(B, ns),
        in_specs=[
            tok(D_MODEL),
            wspec(QKV_W, O_Q), wspec(QKV_W, O_K), wspec(QKV_W, O_V), wspec(ATT_W, O_ZB), wspec(D_MODEL, O_GB),
            bspec(QKV_W, O_Q), bspec(QKV_W, O_K), bspec(QKV_W, O_V), bspec(ATT_W, O_ZB), bspec(D_MODEL, O_GB),
        ] + s_in,
        out_specs=[grp(d) for _, d in GROUPS] + [tok(ATT_W), tok(D_MODEL), s_out_spec],
        out_shape=[jax.ShapeDtypeStruct((B, d, S // d, QKV_W), BF16) for _, d in GROUPS] + [
            jax.ShapeDtypeStruct((B, S, ATT_W), BF16),
            jax.ShapeDtypeStruct((B, S, D_MODEL), BF16),
            s_out_shape,
        ],
        scratch_shapes=[pltpu.VMEM((N_SLAB, tm, LANES), F32)] + s_scratch,
        compiler_params=pltpu.CompilerParams(
            dimension_semantics=("arbitrary", "arbitrary"), vmem_limit_bytes=VMEM_LIMIT),
        name="qkv_prompt",
    )(x, wbf, wbf, wbf, wbf, wbf, bsc, bsc, bsc, bsc, bsc, *s_args)


def _kv_window_kernel(x_ref, wkt_ref, wvt_ref, bkt_ref, bvt_ref, o1_ref, o2_ref, o3_ref, *, tw, nw, keeps):
    j = pl.program_id(1)
    xb = x_ref[...].astype(BF16)
    for g, o_ref in enumerate((o1_ref, o2_ref, o3_ref)):
        rows = min(keeps[g], tw)
        nwt = keeps[g] // rows
        c0 = g * ATT_W

        @pl.when(j >= nw - nwt)
        def _(o_ref=o_ref, rows=rows, c0=c0):
            xw = xb[tw - rows:, :]
            o_ref[0] = _dot_nt(wkt_ref[c0:c0 + ATT_W, :], xw) + bkt_ref[c0:c0 + ATT_W, :]
            o_ref[1] = _dot_nt(wvt_ref[c0:c0 + ATT_W, :], xw) + bvt_ref[c0:c0 + ATT_W, :]


def _kv_window(x, wkt, wvt, bkt, bvt, tw):
    B, S, _ = x.shape
    keeps = tuple(min(w, S) for w, _ in GROUPS)
    nw = max(keeps) // tw
    first = S // tw - nw

    def out_spec(keep):
        rows = min(keep, tw)
        f = nw - keep // rows
        return pl.BlockSpec((None, 2, ATT_W, rows), lambda b, j: (b, 0, 0, jnp.maximum(j - f, 0)))

    return pl.pallas_call(
        functools.partial(_kv_window_kernel, tw=tw, nw=nw, keeps=keeps),
        grid=(B, nw),
        in_specs=[
            pl.BlockSpec((None, tw, D_MODEL), lambda b, j: (b, first + j, 0)),
            _const_spec((QKV_W, D_MODEL), (0, 0)), _const_spec((QKV_W, D_MODEL), (0, 0)),
            _const_spec((QKV_W, 1), (0, 0)), _const_spec((QKV_W, 1), (0, 0)),
        ],
        out_specs=[out_spec(k) for k in keeps],
        out_shape=[jax.ShapeDtypeStruct((B, 2, ATT_W, k), F32) for k in keeps],
        compiler_params=pltpu.CompilerParams(
            dimension_semantics=("arbitrary", "arbitrary"), vmem_limit_bytes=VMEM_LIMIT),
        name="kv_window",
    )(x, wkt, wvt, bkt, bvt)


def _prompt_bias_tables(dil):
    slopes = _alibi_slopes()
    i = np.arange(Q_BLK)[:, None]
    j = np.arange(2 * Q_BLK)[None, :]
    delta = i + Q_BLK - j
    band = (delta >= 0) & (delta <= Q_BLK)
    bias = -slopes[:, None, None] * (delta * dil).astype(np.float32)[None]
    rest = np.where(band[None], bias, np.float32(NEG)).astype(np.float32)
    first = np.where((band & (j >= Q_BLK))[None], bias, np.float32(NEG)).astype(np.float32)
    return np.stack([first, rest]), rest


def _attn_kernel(q_ref, kp_ref, kc_ref, vp_ref, vc_ref, b0_ref, br_ref, o_ref, side_ref, *, nsub, rpb):
    lane = lax.broadcasted_iota(jnp.int32, (Q_BLK, LANES), 1)
    lo = lane < HEAD_DIM
    m_lo = lo.astype(F32).astype(BF16)
    m_hi = (1.0 - lo.astype(F32)).astype(BF16)
    ones = jnp.ones((2 * Q_BLK, LANES), BF16)
    for rr, i in ((rr, i) for rr in range(rpb) for i in range(nsub)):
        r0 = i * Q_BLK
        side = jnp.zeros((Q_BLK, LANES), F32)
        for hp in range(N_SLOTS // 2):
            l0 = hp * LANES
            q2 = q_ref[rr, r0:r0 + Q_BLK, l0:l0 + LANES]
            if i == 0:
                k2 = jnp.concatenate([kp_ref[rr, :, l0:l0 + LANES], kc_ref[rr, 0:Q_BLK, l0:l0 + LANES]], axis=0)
                v2 = jnp.concatenate([vp_ref[rr, :, l0:l0 + LANES], vc_ref[rr, 0:Q_BLK, l0:l0 + LANES]], axis=0)
                bref = b0_ref
            else:
                k2 = kc_ref[rr, r0 - Q_BLK:r0 + Q_BLK, l0:l0 + LANES]
                v2 = vc_ref[rr, r0 - Q_BLK:r0 + Q_BLK, l0:l0 + LANES]
                bref = br_ref
            outs = []
            for par in range(2):
                h = 2 * hp + par
                qm = q2 * (m_lo if par == 0 else m_hi)
                sc = _dot_nt(qm, k2) + bref[h]
                mx = jnp.max(sc, axis=1, keepdims=True)
                p = jnp.exp(sc - mx).astype(BF16)
                oe = _dot(p, jnp.concatenate([v2, ones], axis=1))
                outs.append(oe[:, 0:LANES])
                side = jnp.where(lane == h, mx, side)
                side = jnp.where(lane == N_SLOTS + h, oe[:, LANES:], side)
            o_ref[rr, r0:r0 + Q_BLK, l0:l0 + LANES] = jnp.where(lo, outs[0], outs[1]).astype(BF16)
        side_ref[rr, r0:r0 + Q_BLK, :] = side


ATTN_STEP_ROWS = 2048


def _attn_prompt(qkv, g):
    B, dil, L, _ = qkv.shape
    qb = min(ATTN_STEP_ROWS, L)
    nq = L // qb
    nsub = qb // Q_BLK
    rpb = min(dil, max(1, ATTN_STEP_ROWS // qb))
    b0, br = _prompt_bias_tables(dil)
    cur = lambda part: pl.BlockSpec((None, rpb, qb, ATT_W), lambda b, r, n: (b, r, n, part))
    prev = lambda part: pl.BlockSpec((None, rpb, Q_BLK, ATT_W),
                                     lambda b, r, n: (b, r, jnp.maximum(n * nsub - 1, 0), part))
    return pl.pallas_call(
        functools.partial(_attn_kernel, nsub=nsub, rpb=rpb),
        grid=(B, dil // rpb, nq),
        in_specs=[
            cur(0), prev(1), cur(1), prev(2), cur(2),
            pl.BlockSpec((None, N_SLOTS, Q_BLK, 2 * Q_BLK), lambda b, r, n: (jnp.minimum(n, 1), 0, 0, 0)),
            _const_spec((N_SLOTS, Q_BLK, 2 * Q_BLK), (0, 0, 0)),
        ],
        out_specs=[
            pl.BlockSpec((None, rpb, qb, ATT_W), lambda b, r, n: (b, r, n, 0)),
            pl.BlockSpec((None, rpb, qb, LANES), lambda b, r, n: (b, r, n, 0)),
        ],
        out_shape=[
            jax.ShapeDtypeStruct((B, dil, L, ATT_W), BF16),
            jax.ShapeDtypeStruct((B, dil, L, LANES), F32),
        ],
        compiler_params=pltpu.CompilerParams(
            dimension_semantics=("arbitrary", "arbitrary", "arbitrary"), vmem_limit_bytes=VMEM_LIMIT),
        name=f"attn_prompt_g{g}",
    )(qkv, qkv, qkv, qkv, qkv, jnp.asarray(b0), jnp.asarray(br))


def _merge_tail(x, gapa, sgb, zb, attn, wb_ref, wo_ref, g_ref, bt_ref):
    ab = (attn * zb.astype(F32)).astype(BF16)
    pb = _dot(ab, wb_ref[...])
    m = gapa.astype(F32) + sgb.astype(F32) * pb
    o = _dot(m.astype(BF16), wo_ref[...])
    return _layer_norm(ALPHA * x + o, g_ref[...], bt_ref[...])


def _merge_groups_kernel(x_ref, gapa_ref, sgb_ref, zb_ref, o1_ref, o2_ref, o3_ref, s1_ref, s2_ref, s3_ref,
                         ex_ref, wb_ref, wo_ref, g_ref, bt_ref, y_ref, nat_ref, nats_ref, *, tm):
    nsl = ATT_W // LANES
    for gi, (o_ref, s_ref) in enumerate(((o2_ref, s2_ref), (o3_ref, s3_ref))):
        dil = GROUPS[gi + 1][1]
        n = tm // dil
        for r in range(dil):
            blk = o_ref[r].astype(F32)
            for sl in range(nsl):
                nat_ref[gi, sl, pl.ds(r, n, stride=dil), :] = blk[:, sl * LANES:(sl + 1) * LANES]
            nats_ref[gi, pl.ds(r, n, stride=dil), :] = s_ref[r]
    sides = (s1_ref[...], nats_ref[0], nats_ref[1])
    mx = jnp.maximum(jnp.maximum(sides[0], sides[1]), sides[2])
    es = [jnp.exp(t - mx) for t in sides]
    dens = [pltpu.roll(t, LANES - N_SLOTS, axis=1) for t in sides]
    total = es[0] * dens[0] + es[1] * dens[1] + es[2] * dens[2]
    head_lane = lax.broadcasted_iota(jnp.int32, total.shape, 1) < N_SLOTS
    inv = jnp.where(head_lane, 1.0 / total, 0.0)
    ex = ex_ref[...]
    w1, w2, w3 = (_dot((e * inv).astype(BF16), ex) for e in es)
    slabs = []
    for sl in range(nsl):
        cs = slice(sl * LANES, (sl + 1) * LANES)
        slabs.append(w1[:, cs] * o1_ref[:, cs].astype(F32) + w2[:, cs] * nat_ref[0, sl] + w3[:, cs] * nat_ref[1, sl])
    attn = jnp.concatenate(slabs, axis=1)
    y_ref[...] = _merge_tail(x_ref[...], gapa_ref[...], sgb_ref[...], zb_ref[...], attn,
                             wb_ref, wo_ref, g_ref, bt_ref)


def _merge_plain_kernel(x_ref, gapa_ref, sgb_ref, zb_ref, attn_ref, wb_ref, wo_ref, g_ref, bt_ref, y_ref):
    y_ref[...] = _merge_tail(x_ref[...], gapa_ref[...], sgb_ref[...], zb_ref[...], attn_ref[...],
                             wb_ref, wo_ref, g_ref, bt_ref)


def _head_expand_matrix():
    ex = np.zeros((LANES, ATT_W), np.float32)
    for h in range(N_SLOTS):
        ex[h, h * HEAD_DIM:(h + 1) * HEAD_DIM] = 1.0
    return ex


def _merge_weight_specs():
    return [_const_spec((ATT_W, D_MODEL), (0, 0)), _const_spec((D_MODEL, D_MODEL), (0, 0)),
            _const_spec((1, D_MODEL), (0, 0)), _const_spec((1, D_MODEL), (0, 0))]


MERGE_ROWS = 1024


def _merge_groups(x, gapa, sgb, zb, parts, wb_bf, wo_bf, ln_g, ln_b, tm):
    B, S, _ = x.shape
    assert S % tm == 0 and tm % (GROUPS[-1][1] * 8) == 0
    tok = lambda width: pl.BlockSpec((None, tm, width), lambda b, s: (b, s, 0))
    grp = lambda dil, width: pl.BlockSpec((None, dil, tm // dil, width), lambda b, s: (b, 0, s, 0))
    (o1, s1), (o2, s2), (o3, s3) = parts
    d2, d3 = GROUPS[1][1], GROUPS[2][1]
    return pl.pallas_call(
        functools.partial(_merge_groups_kernel, tm=tm),
        grid=(B, S // tm),
        in_specs=[tok(D_MODEL), tok(D_MODEL), tok(D_MODEL), tok(ATT_W),
                  tok(ATT_W), grp(d2, ATT_W), grp(d3, ATT_W), tok(LANES), grp(d2, LANES), grp(d3, LANES),
                  _const_spec((LANES, ATT_W), (0, 0))] + _merge_weight_specs(),
        out_specs=tok(D_MODEL),
        out_shape=jax.ShapeDtypeStruct((B, S, D_MODEL), F32),
        scratch_shapes=[pltpu.VMEM((2, ATT_W // LANES, tm, LANES), F32), pltpu.VMEM((2, tm, LANES), F32)],
        compiler_params=pltpu.CompilerParams(
            dimension_semantics=("arbitrary", "arbitrary"), vmem_limit_bytes=VMEM_LIMIT),
        name="merge_groups",
    )(x, gapa, sgb, zb, o1.reshape(B, S, ATT_W), o2, o3, s1.reshape(B, S, LANES), s2, s3,
      jnp.asarray(_head_expand_matrix(), BF16), wb_bf, wo_bf, ln_g, ln_b)


def _merge_plain(x2, gapa, sgb, zb, attn, wb_bf, wo_bf, ln_g, ln_b):
    T = x2.shape[0]
    tok = lambda width: pl.BlockSpec((T, width), lambda t: (0, 0))
    return pl.pallas_call(
        _merge_plain_kernel,
        grid=(1,),
        in_specs=[tok(D_MODEL), tok(D_MODEL), tok(D_MODEL), tok(ATT_W), tok(ATT_W)] + _merge_weight_specs(),
        out_specs=tok(D_MODEL),
        out_shape=jax.ShapeDtypeStruct((T, D_MODEL), F32),
        compiler_params=pltpu.CompilerParams(dimension_semantics=("arbitrary",), vmem_limit_bytes=VMEM_LIMIT),
        name="merge_plain",
    )(x2, gapa, sgb, zb, attn, wb_bf, wo_bf, ln_g, ln_b)


def _sample_proj_kernel(x_ref, w_ref, b_ref, h_ref):
    h_ref[...] = _dot(x_ref[...].astype(BF16), w_ref[...]) + b_ref[...]


def _sample_proj(xs2, wbf, bsc):
    T = xs2.shape[0]
    cw = SAMPLE_PROJ_COLS
    return pl.pallas_call(
        _sample_proj_kernel,
        grid=(IN_W // cw,),
        in_specs=[
            pl.BlockSpec((T, D_MODEL), lambda j: (0, 0)),
            pl.BlockSpec((D_MODEL, cw), lambda j: (0, j)),
            pl.BlockSpec((1, cw), lambda j: (0, j)),
        ],
        out_specs=pl.BlockSpec((T, cw), lambda j: (0, j)),
        out_shape=jax.ShapeDtypeStruct((T, IN_W), F32),
        compiler_params=pltpu.CompilerParams(dimension_semantics=("arbitrary",), vmem_limit_bytes=VMEM_LIMIT),
        name="sample_proj",
    )(xs2, wbf, bsc)


def _sample_conv_kernel(h3_ref, hga_ref, hzb_ref, hgb_ref, st_ref, cw_ref, cb_ref, lg_ref, lb_ref, wa_ref,
                        gapa_ref, zb_ref, sgb_ref, cst_ref, hist_ref, slab_ref, cvt_ref, *, nb, tq):
    nh = CONV_WIDTH - 1
    a = h3_ref[:, O_AVAL:O_AVAL + CONV_CH]
    gt = h3_ref[:, O_AGATE:O_AGATE + CONV_CH]
    u = a * _sigmoid(gt)
    for sl in range(N_SLAB):
        slab_ref[sl] = u[:, sl * LANES:(sl + 1) * LANES]
    hist_ref[0:nh] = st_ref[...]
    for t in range(tq):
        hist_ref[nh + t] = jnp.concatenate(
            [slab_ref[sl, pl.ds(t, nb, stride=tq), :] for sl in range(N_SLAB)], axis=1)

    def conv_step(t, carry):
        acc = jnp.zeros((nb, CONV_CH), F32)
        for j in range(CONV_WIDTH):
            wj = jnp.concatenate([cw_ref[j * 8:(j + 1) * 8, :]] * (nb // 8), axis=0)
            acc = acc + hist_ref[t + j] * wj
        cvt_ref[t] = acc + cb_ref[...]
        return carry

    lax.fori_loop(0, tq, conv_step, 0)
    for t in range(tq):
        cvt = cvt_ref[t]
        for sl in range(N_SLAB):
            slab_ref[sl, pl.ds(t, nb, stride=tq), :] = cvt[:, sl * LANES:(sl + 1) * LANES]
    cst_ref[...] = hist_ref[tq:tq + nh]

    cv = jnp.concatenate([slab_ref[sl] for sl in range(N_SLAB)], axis=1)
    y = _layer_norm(cv, lg_ref[...], lb_ref[...])
    ca = (_silu(y) * _silu(h3_ref[:, O_ZA:O_ZA + CONV_CH])).astype(BF16)
    gapa_ref[...] = (_sigmoid(hga_ref[...]) * _dot(ca, wa_ref[...])).astype(BF16)
    zb_ref[...] = _silu(hzb_ref[...]).astype(BF16)
    sgb_ref[...] = _sigmoid(hgb_ref[...]).astype(BF16)


def _sample_conv_branch(hs, state_t, cw8, conv_b, lg, lb, wa_bf, nb, tq):
    T = nb * tq
    nh = CONV_WIDTH - 1
    col = lambda width, off: pl.BlockSpec((T, width), lambda i: (0, off // width))
    full = lambda shape: pl.BlockSpec(shape, lambda i: (0,) * len(shape))
    return pl.pallas_call(
        functools.partial(_sample_conv_kernel, nb=nb, tq=tq),
        grid=(1,),
        in_specs=[
            col(3 * CONV_CH, 0), col(D_MODEL, O_GA), col(ATT_W, O_ZB), col(D_MODEL, O_GB),
            full((nh, nb, CONV_CH)), full((CONV_WIDTH * 8, CONV_CH)),
            full((1, CONV_CH)), full((1, CONV_CH)), full((1, CONV_CH)), full((CONV_CH, D_MODEL)),
        ],
        out_specs=[full((T, D_MODEL)), full((T, ATT_W)), full((T, D_MODEL)), full((nh, nb, CONV_CH))],
        out_shape=[
            jax.ShapeDtypeStruct((T, D_MODEL), BF16),
            jax.ShapeDtypeStruct((T, ATT_W), BF16),
            jax.ShapeDtypeStruct((T, D_MODEL), BF16),
            jax.ShapeDtypeStruct((nh, nb, CONV_CH), F32),
        ],
        scratch_shapes=[pltpu.VMEM((nh + tq, nb, CONV_CH), F32), pltpu.VMEM((N_SLAB, T, LANES), F32),
                        pltpu.VMEM((tq, nb, CONV_CH), F32)],
        compiler_params=pltpu.CompilerParams(dimension_semantics=("arbitrary",), vmem_limit_bytes=VMEM_LIMIT),
        name="sample_conv_branch",
    )(hs, hs, hs, hs, state_t, cw8, conv_b, lg, lb, wa_bf)


def _sample_bias_tables(g, tq):
    win, dil = GROUPS[g]
    slopes = _alibi_slopes()
    tabc = np.full((N_SLOTS * tq, win), NEG, np.float32)
    tabn = np.full((N_SLOTS * tq, tq), NEG, np.float32)
    pos = np.arange(win)
    for h in range(N_SLOTS):
        for t in range(tq):
            row = h * tq + t
            dist = win + t - pos
            ok = (dist % dil == 0) & (dist // dil <= win // dil)
            tabc[row, ok] = -slopes[h] * dist[ok].astype(np.float32)
            for t2 in range(t + 1):
                if (t - t2) % dil == 0:
                    tabn[row, t2] = -slopes[h] * np.float32(t - t2)
    return tabc, tabn


def _sample_attn_step(kv, q_ref, k_ref, v_ref, c1_ref, c2_ref, c3_ref, tc1_ref, tc2_ref, tc3_ref,
                      tn1_ref, tn2_ref, tn3_ref, hm_ref, o_ref,
                      p1_ref, p2_ref, p3_ref, pn_ref, den_ref, *, tq):
    caches = (c1_ref, c2_ref, c3_ref)
    probs = (p1_ref, p2_ref, p3_ref)
    tabc = (tc1_ref, tc2_ref, tc3_ref)
    tabn = (tn1_ref, tn2_ref, tn3_ref)
    hm = hm_ref[...]

    @pl.when(kv == 0)
    def _():
        scores, mx = [], None
        for g in range(N_GROUPS):
            c0 = g * ATT_W
            qbd = (jnp.concatenate([q_ref[:, c0:c0 + ATT_W]] * N_SLOTS, axis=0) * hm).astype(BF16)
            sc = _dot(qbd, caches[g][...].astype(BF16)) + tabc[g][...]
            sn = _dot_nt(qbd, k_ref[:, c0:c0 + ATT_W].astype(BF16)) + tabn[g][...]
            m = jnp.maximum(jnp.max(sc, axis=1, keepdims=True), jnp.max(sn, axis=1, keepdims=True))
            mx = m if mx is None else jnp.maximum(mx, m)
            scores.append((sc, sn))
        den = jnp.zeros((N_SLOTS * tq, 1), F32)
        for g in range(N_GROUPS):
            sc, sn = scores[g]
            p = jnp.exp(sc - mx)
            pn = jnp.exp(sn - mx)
            den = den + jnp.sum(p, axis=1, keepdims=True) + jnp.sum(pn, axis=1, keepdims=True)
            probs[g][...] = p.astype(BF16)
            pn_ref[g] = pn
        den_ref[...] = den

    @pl.when(kv == 1)
    def _():
        acc = jnp.zeros((N_SLOTS * tq, ATT_W), F32)
        for g in range(N_GROUPS):
            c0 = g * ATT_W
            acc = acc + _dot_nt(probs[g][...], caches[g][...].astype(BF16))
            acc = acc + _dot(pn_ref[g].astype(BF16), v_ref[:, c0:c0 + ATT_W].astype(BF16))
        out = (acc / den_ref[...]) * hm
        o_ref[...] = jnp.sum(out.reshape(N_SLOTS, tq, ATT_W), axis=0)


def _sample_attn_operands(hs3, caches_t, nb, tq, step_of):
    seq = lambda *ids: step_of(*ids) // 2
    half = lambda *ids: step_of(*ids) % 2
    cspecs, tcs, tns = [], [], []
    for g, (win, _) in enumerate(GROUPS):
        cspecs.append(pl.BlockSpec((None, None, ATT_W, win), lambda *ids: (seq(*ids), half(*ids), 0, 0)))
        tc, tn = _sample_bias_tables(g, tq)
        tcs.append(jnp.asarray(tc))
        tns.append(jnp.asarray(tn))
    hm = np.zeros((N_SLOTS * tq, ATT_W), np.float32)
    for h in range(N_SLOTS):
        hm[h * tq:(h + 1) * tq, h * HEAD_DIM:(h + 1) * HEAD_DIM] = 1.0
    qkv = lambda off: pl.BlockSpec((None, tq, QKV_W), lambda *ids: (seq(*ids), 0, off // QKV_W))
    const = lambda a: _const_spec(a.shape, (0,) * a.ndim)
    nrow = N_SLOTS * tq
    in_specs = ([qkv(O_Q), qkv(O_K), qkv(O_V)] + cspecs + [const(t) for t in tcs] + [const(t) for t in tns]
                + [_const_spec(hm.shape, (0, 0))])
    args = [hs3, hs3, hs3, *caches_t, *tcs, *tns, jnp.asarray(hm)]
    out_spec = pl.BlockSpec((None, tq, ATT_W), lambda *ids: (seq(*ids), 0, 0))
    out_shape = jax.ShapeDtypeStruct((nb, tq, ATT_W), F32)
    scratch = ([pltpu.VMEM((nrow, win), BF16) for win, _ in GROUPS]
               + [pltpu.VMEM((N_GROUPS, nrow, tq), F32), pltpu.VMEM((nrow, 1), F32)])
    return in_specs, args, out_spec, out_shape, scratch


def _to_feature_major(c):
    n, w = c.shape[0], c.shape[1]
    return jnp.transpose(c, (0, 2, 3, 4, 1)).reshape(n, 2, ATT_W, w)


def _from_feature_major(c):
    n, _, _, w = c.shape
    return jnp.transpose(c.reshape(n, 2, N_SLOTS, HEAD_DIM, w), (0, 4, 1, 2, 3))


def kernel(x_prompt, x_sample, cache_kv_w128, cache_kv_w512, cache_kv_w2048, state_conv, w_in, b_in, conv_w, conv_b,
           conv_ln_g, conv_ln_b, w_a, w_b, w_out, ln_g, ln_b):
    B, S, _ = x_prompt.shape
    nb, tq, _ = x_sample.shape
    caches = (cache_kv_w128, cache_kv_w512, cache_kv_w2048)
    tm = PROMPT_ROWS
    for (win, dil), c in zip(GROUPS, caches):
        assert c.shape[1] == win and S % (dil * Q_BLK) == 0 and S >= win and tm % (dil * 16) == 0
    assert tq == 8 and nb % 8 == 0 and state_conv.shape[1] == CONV_WIDTH - 1 and S % tm == 0

    col_scale = np.ones((IN_W,), np.float32)
    col_scale[O_Q:O_K] = HEAD_DIM ** -0.5
    wbf = (w_in * col_scale).astype(BF16)
    b1 = b_in * col_scale
    bsc = b1.reshape(1, IN_W)
    wkvt = lax.optimization_barrier(w_in[:, O_K:O_ZB]).T.astype(BF16)
    wkt, wvt = wkvt[:QKV_W], wkvt[QKV_W:]
    bkt, bvt = b1[O_K:O_V].reshape(QKV_W, 1), b1[O_V:O_ZB].reshape(QKV_W, 1)
    cw8 = jnp.broadcast_to(conv_w[:, None, :], (CONV_WIDTH, 8, CONV_CH)).reshape(CONV_WIDTH * 8, CONV_CH)
    cb, clg, clb = (t.reshape(1, CONV_CH) for t in (conv_b, conv_ln_g, conv_ln_b))
    lg, lb = ln_g.reshape(1, D_MODEL), ln_b.reshape(1, D_MODEL)
    wa_bf, wb_bf, wo_bf = w_a.astype(BF16), w_b.astype(BF16), w_out.astype(BF16)

    Ts = nb * tq
    hs = _sample_proj(x_sample.reshape(Ts, D_MODEL), wbf, bsc)
    hs3 = hs.reshape(nb, tq, IN_W)
    kv_new = jnp.stack([hs3[:, :, O_K:O_V], hs3[:, :, O_V:O_ZB]], axis=1)
    caches_t = [_to_feature_major(c) for c in caches]

    gapa, cst_p, *kvs_rows = _conv_branch_prompt(x_prompt, wbf, bsc, cw8, cb, clg, clb, wa_bf, kv_new, caches_t, tm)
    qkv1, qkv2, qkv3, zb, sgb, attn_s = _qkv_prompt(x_prompt, wbf, bsc, hs3, caches_t, tm)
    kvt = _kv_window(x_prompt, wkt, wvt, bkt, bvt, KV_WINDOW_ROWS)
    parts = [_attn_prompt(t, g) for g, t in enumerate((qkv1, qkv2, qkv3))]
    y_p = _merge_groups(x_prompt, gapa, sgb, zb, parts, wb_bf, wo_bf, lg, lb, MERGE_ROWS)
    kv_p = [_from_feature_major(t) for t in kvt]
    conv_p = cst_p[:, HOFF:, :]

    state_t = jnp.transpose(state_conv, (1, 0, 2))
    gapa_s, zb_s, sgb_s, cst_t = _sample_conv_branch(hs, state_t, cw8, cb, clg, clb, wa_bf, nb, tq)
    y_s = _merge_plain(x_sample.reshape(Ts, D_MODEL), gapa_s, sgb_s, zb_s, attn_s.reshape(Ts, ATT_W),
                       wb_bf, wo_bf, lg, lb).reshape(nb, tq, D_MODEL)
    kv_s = [_from_feature_major(t.reshape(nb, 2, ATT_W, t.shape[-1])) for t in kvs_rows]
    conv_s = jnp.transpose(cst_t, (1, 0, 2))
    return (y_p, y_s, kv_p[0], kv_p[1], kv_p[2], conv_p, kv_s[0], kv_s[1], kv_s[2], conv_s)
```
